```python
import jax
import jax.numpy as jnp
from jax import lax
import numpy as np

D_MODEL = 4096
BATCH = 2
SEQ = 4096
DEPTH = 2

HEAD_DIM = 128
A_HEADS = (D_MODEL // 2) // HEAD_DIM
A_WIDTH = A_HEADS * HEAD_DIM
A_PATTERNS = ((128, 1), (512, 4), (2048, 16))
B_WIDTH = D_MODEL - A_WIDTH
B_HEADS = 4
B_V_DIM = B_WIDTH // B_HEADS
B_QK_DIM = B_V_DIM // 2
B_QK_WIDTH = B_HEADS * B_QK_DIM
B_CONV_WIDTH = 4
B_CHUNK = 64
C_HEADS = (D_MODEL // 2) // HEAD_DIM
C_WIDTH = C_HEADS * HEAD_DIM
MOBA_BLOCK = 256
MOBA_TOPK = 3
MOBA_QUERY_CHUNK = 16
D_CHANNELS = D_MODEL - C_WIDTH
D_CONV_WIDTH = 31
D_FF = 2 * D_MODEL
N_SUBLAYERS = 3
ADA_WIDTH = 3 * N_SUBLAYERS * D_MODEL
AB_IN_WIDTH = 3 * A_WIDTH + 2 * B_QK_WIDTH + 2 * B_WIDTH + 2 * B_HEADS
CD_IN_WIDTH = 3 * C_WIDTH + 2 * D_CHANNELS
DEEPNORM_ALPHA = (2 * DEPTH) ** 0.25
DEEPNORM_BETA = (8 * DEPTH) ** -0.25
LN_EPS = 1e-5

kernel_name = 'hybrid_dilated_mlstm_moba_conv_block'


def layer_norm(x, g, b):
    xf = x.astype(jnp.float32)
    mu = xf.mean(-1, keepdims=True)
    var = jnp.square(xf - mu).mean(-1, keepdims=True)
    return ((xf - mu) * lax.rsqrt(var + LN_EPS)).astype(x.dtype) * g + b


def split_cols(t, sizes):
    out, start = [], 0
    for n in sizes:
        out.append(t[..., start:start + n])
        start += n
    return out


def split_heads(t, n_heads):
    b, s, _ = t.shape
    return t.reshape(b, s, n_heads, -1).transpose(0, 2, 1, 3)


def merge_heads(t):
    b, h, s, d = t.shape
    return t.transpose(0, 2, 1, 3).reshape(b, s, h * d)


def alibi_slopes(n_heads):
    return jnp.asarray(2.0 ** (-8.0 * np.arange(1, n_heads + 1) / n_heads), jnp.float32)


def causal_depthwise_conv(x, w):
    width, ch = w.shape
    xp = jnp.pad(x, ((0, 0), (width - 1, 0), (0, 0)))
    return lax.conv_general_dilated(xp, w[:, None, :], window_strides=(1,), padding='VALID',
                                    dimension_numbers=('NWC', 'WIO', 'NWC'), feature_group_count=ch)


def swiglu_ffn(u, w_up, w_down):
    g, v = jnp.split(u @ w_up, 2, axis=-1)
    return (jax.nn.silu(g) * v) @ w_down


def dilated_window_branch(q, k, v, slopes, window, dilation):
    b, h, s, dh = q.shape
    steps = window // dilation
    unit = dilation * steps
    p = -(-s // unit) * unit
    n_blk = p // unit

    def to_strided(t):
        t = jnp.pad(t, ((0, 0), (0, 0), (0, p - s), (0, 0)))
        t = t.reshape(b, h, p // dilation, dilation, dh).transpose(0, 1, 3, 2, 4)
        return t.reshape(b, h, dilation, n_blk, steps, dh)

    def with_prev(t):
        prev = jnp.pad(t, ((0, 0), (0, 0), (0, 0), (1, 0), (0, 0), (0, 0)))[:, :, :, :-1]
        return jnp.concatenate([prev, t], axis=4)

    qs = to_strided(q)
    ks = with_prev(to_strided(k))
    vs = with_prev(to_strided(v))
    scores = jnp.einsum('bhrnqd,bhrnkd->bhrnqk', qs, ks).astype(jnp.float32)
    step = steps + jnp.arange(steps)[:, None] - jnp.arange(2 * steps)[None, :]
    in_band = (step >= 0) & (step <= steps)
    after_start = (jnp.arange(n_blk)[:, None, None] > 0) | (jnp.arange(2 * steps)[None, None, :] >= steps)
    mask = in_band[None] & after_start
    dist = (step * dilation).astype(jnp.float32)
    scores = jnp.where(mask, scores - slopes[:, None, None, None, None] * dist, -jnp.inf)
    row_max = scores.max(-1, keepdims=True)
    probs = jnp.exp(scores - row_max)
    denom = probs.sum(-1)
    out = jnp.einsum('bhrnqk,bhrnkd->bhrnqd', probs, vs) / denom[..., None]

    def from_strided(t):
        t = t.reshape(b, h, dilation, p // dilation, *t.shape[5:])
        t = jnp.moveaxis(t, 2, 3)
        return t.reshape(b, h, p, *t.shape[4:])[:, :, :s]

    return from_strided(out), from_strided(row_max[..., 0]), from_strided(denom)


def dilated_attention(q, k, v, slopes):
    branches = [dilated_window_branch(q, k, v, slopes, w, d) for (w, d) in A_PATTERNS]
    outs = jnp.stack([o for o, _, _ in branches])
    maxes = jnp.stack([m for _, m, _ in branches])
    denoms = jnp.stack([z for _, _, z in branches])
    weight = denoms * jnp.exp(maxes - maxes.max(0, keepdims=True))
    return (weight[..., None] * outs).sum(0) / weight.sum(0)[..., None]


def mlstm_chunkwise(q, k, v, i_pre, f_pre):
    b, h, s, dk = q.shape
    dv = v.shape[-1]
    nc = s // B_CHUNK

    def chunks(t):
        return jnp.moveaxis(t.reshape(b, h, nc, B_CHUNK, *t.shape[3:]), 2, 0)

    causal = jnp.tril(jnp.ones((B_CHUNK, B_CHUNK), bool))

    def step(carry, xs):
        c_state, n_state, m_state = carry
        qc, kc, vc, ic, lfc = xs
        cum = jnp.cumsum(lfc, axis=-1)
        log_w = jnp.where(causal, cum[..., :, None] - cum[..., None, :] + ic[..., None, :], -jnp.inf)
        log_inter = cum + m_state[..., None]
        m_row = jnp.maximum(log_inter, log_w.max(-1))
        w_intra = jnp.exp(log_w - m_row[..., None])
        w_inter = jnp.exp(log_inter - m_row)
        attn = w_intra * jnp.einsum('bhtd,bhsd->bhts', qc, kc)
        num = w_inter[..., None] * jnp.einsum('bhtd,bhde->bhte', qc, c_state) + jnp.einsum('bhts,bhse->bhte', attn, vc)
        den = w_inter * jnp.einsum('bhtd,bhd->bht', qc, n_state) + attn.sum(-1)
        h_out = num / jnp.maximum(jnp.abs(den), jnp.exp(-m_row))[..., None]
        log_to_end = cum[..., -1:] - cum + ic
        m_new = jnp.maximum(cum[..., -1] + m_state, log_to_end.max(-1))
        w_end = jnp.exp(log_to_end - m_new[..., None])
        decay = jnp.exp(cum[..., -1] + m_state - m_new)
        c_new = decay[..., None, None] * c_state + jnp.einsum('bhs,bhsd,bhse->bhde', w_end, kc, vc)
        n_new = decay[..., None] * n_state + jnp.einsum('bhs,bhsd->bhd', w_end, kc)
        return (c_new, n_new, m_new), h_out

    f32 = jnp.float32
    init = (jnp.zeros((b, h, dk, dv), f32), jnp.zeros((b, h, dk), f32), jnp.zeros((b, h), f32))
    xs = (chunks(q), chunks(k), chunks(v), chunks(i_pre), chunks(jax.nn.log_sigmoid(f_pre)))
    _, hs = lax.scan(step, init, xs)
    return jnp.moveaxis(hs, 0, 2).reshape(b, h, s, dv)


def mixer_dilated_mlstm(u, w_in, conv_qk, b_igate, b_fgate, norm_g, w_out):
    a_q, a_k, a_v, b_qk, b_v, b_o, b_i, b_f = split_cols(
        u @ w_in, (A_WIDTH, A_WIDTH, A_WIDTH, 2 * B_QK_WIDTH, B_WIDTH, B_WIDTH, B_HEADS, B_HEADS))
    attn = dilated_attention(split_heads(a_q, A_HEADS) * HEAD_DIM ** -0.5, split_heads(a_k, A_HEADS),
                             split_heads(a_v, A_HEADS), alibi_slopes(A_HEADS))
    a_out = merge_heads(attn).astype(u.dtype)
    b_q, b_k = split_cols(jax.nn.silu(causal_depthwise_conv(b_qk, conv_qk)), (B_QK_WIDTH, B_QK_WIDTH))
    f32 = jnp.float32
    h_cell = mlstm_chunkwise(
        split_heads(b_q, B_HEADS).astype(f32),
        split_heads(b_k, B_HEADS).astype(f32) * B_QK_DIM ** -0.5,
        split_heads(b_v, B_HEADS).astype(f32),
        (b_i + b_igate).astype(f32).transpose(0, 2, 1),
        (b_f + b_fgate).astype(f32).transpose(0, 2, 1))
    mu = h_cell.mean(-1, keepdims=True)
    var = jnp.square(h_cell - mu).mean(-1, keepdims=True)
    h_norm = merge_heads((h_cell - mu) * lax.rsqrt(var + LN_EPS)).astype(u.dtype) * norm_g
    b_out = jax.nn.sigmoid(b_o) * h_norm
    return jnp.concatenate([a_out, b_out], axis=-1) @ w_out


def moba_attention(q, k, v, slopes):
    b, h, s, dh = q.shape
    p = -(-s // MOBA_BLOCK) * MOBA_BLOCK
    n_blk = p // MOBA_BLOCK
    top_k = min(MOBA_TOPK, n_blk)
    pad = ((0, 0), (0, 0), (0, p - s), (0, 0))
    q, k, v = (jnp.pad(t, pad) for t in (q, k, v))
    k_blocks = k.reshape(b, h, n_blk, MOBA_BLOCK, dh)
    v_blocks = v.reshape(b, h, n_blk, MOBA_BLOCK, dh)
    gate = jnp.einsum('bhsd,bhnd->bhsn', q, k_blocks.mean(3)).astype(jnp.float32)
    q_blk = jnp.arange(p) // MOBA_BLOCK
    gate = jnp.where(jnp.arange(n_blk)[None, :] < q_blk[:, None], gate, -jnp.inf)
    _, sel = lax.top_k(gate, top_k)
    sel_valid = jnp.arange(top_k)[None, :] < q_blk[:, None]
    n_chunks = p // MOBA_QUERY_CHUNK

    def chunk_t(t):
        return jnp.moveaxis(t.reshape(b, h, n_chunks, MOBA_QUERY_CHUNK, *t.shape[3:]), 2, 0)

    xs = (chunk_t(q), chunk_t(sel), sel_valid.reshape(n_chunks, MOBA_QUERY_CHUNK, top_k),
          jnp.arange(p).reshape(n_chunks, MOBA_QUERY_CHUNK))
    bi = jnp.arange(b)[:, None, None, None]
    hi = jnp.arange(h)[None, :, None, None]
    key_off = jnp.arange(MOBA_BLOCK)

    def attend(args):
        qc, selc, validc, pos = args
        k_sel = k_blocks[bi, hi, selc]
        v_sel = v_blocks[bi, hi, selc]
        own = pos[0] // MOBA_BLOCK
        k_own = lax.dynamic_index_in_dim(k_blocks, own, axis=2, keepdims=False)
        v_own = lax.dynamic_index_in_dim(v_blocks, own, axis=2, keepdims=False)
        s_sel = jnp.einsum('bhqd,bhqjkd->bhqjk', qc, k_sel).astype(jnp.float32)
        dist_sel = (pos[:, None, None] - (selc[..., None] * MOBA_BLOCK + key_off)).astype(jnp.float32)
        s_sel = jnp.where(validc[:, :, None], s_sel - slopes[:, None, None, None] * dist_sel, -jnp.inf)
        dist_own = (pos[:, None] - (own * MOBA_BLOCK + key_off)[None, :]).astype(jnp.float32)
        s_own = jnp.einsum('bhqd,bhkd->bhqk', qc, k_own).astype(jnp.float32)
        s_own = jnp.where(dist_own >= 0, s_own - slopes[:, None, None] * dist_own, -jnp.inf)
        n_sel = top_k * MOBA_BLOCK
        probs = jax.nn.softmax(jnp.concatenate([s_sel.reshape(b, h, MOBA_QUERY_CHUNK, n_sel), s_own], axis=-1), axis=-1)
        p_sel = probs[..., :n_sel].reshape(b, h, MOBA_QUERY_CHUNK, top_k, MOBA_BLOCK)
        return (jnp.einsum('bhqjk,bhqjkd->bhqd', p_sel, v_sel)
                + jnp.einsum('bhqk,bhkd->bhqd', probs[..., n_sel:], v_own))

    out = lax.map(attend, xs)
    return jnp.moveaxis(out, 0, 2).reshape(b, h, p, dh)[:, :, :s]


def mixer_moba_conv(u, w_in, conv_dw, conv_ln_g, conv_ln_b, w_out):
    c_q, c_k, c_v, d_a, d_b = split_cols(u @ w_in, (C_WIDTH, C_WIDTH, C_WIDTH, D_CHANNELS, D_CHANNELS))
    attn = moba_attention(split_heads(c_q, C_HEADS) * HEAD_DIM ** -0.5, split_heads(c_k, C_HEADS),
                          split_heads(c_v, C_HEADS), alibi_slopes(C_HEADS))
    c_out = merge_heads(attn).astype(u.dtype)
    d = causal_depthwise_conv(d_a * jax.nn.sigmoid(d_b), conv_dw)
    d_out = jax.nn.silu(layer_norm(d, conv_ln_g, conv_ln_b))
    return jnp.concatenate([c_out, d_out], axis=-1) @ w_out


def setup_inputs(seed: int = 0) -> dict:
    key = jax.random.key(seed)
    keys = iter(jax.random.split(key, 64))

    def normal(shape, scale):
        return jax.random.normal(next(keys), shape, jnp.float32) * scale

    def gain(shape):
        return 1.0 + normal(shape, 0.02)

    inputs = {'x': normal((BATCH, SEQ, D_MODEL), 1.0), 'c': normal((BATCH, D_MODEL), 1.0)}
    for layer in range(DEPTH):
        p = 'l%d_' % layer
        inputs[p + 'w_ada'] = normal((D_MODEL, ADA_WIDTH), 0.1 * D_MODEL ** -0.5)
        inputs[p + 'b_ada'] = normal((ADA_WIDTH,), 0.01)
        inputs[p + 'ln_g'] = gain((N_SUBLAYERS, D_MODEL))
        inputs[p + 'ln_b'] = normal((N_SUBLAYERS, D_MODEL), 0.02)
        for f in ('ffn1_', 'ffn2_'):
            inputs[p + f + 'w_up'] = normal((D_MODEL, 2 * D_FF), D_MODEL ** -0.5)
            inputs[p + f + 'w_down'] = normal((D_FF, D_MODEL), DEEPNORM_BETA * D_FF ** -0.5)
        if layer % 2 == 0:
            inputs[p + 'w_in'] = normal((D_MODEL, AB_IN_WIDTH), D_MODEL ** -0.5)
            inputs[p + 'conv_qk'] = normal((B_CONV_WIDTH, 2 * B_QK_WIDTH), B_CONV_WIDTH ** -0.5)
            inputs[p + 'b_igate'] = normal((B_HEADS,), 0.1)
            inputs[p + 'b_fgate'] = 3.0 + 3.0 * jax.random.uniform(next(keys), (B_HEADS,), jnp.float32)
            inputs[p + 'mlstm_norm_g'] = gain((B_WIDTH,))
            inputs[p + 'w_out'] = normal((A_WIDTH + B_WIDTH, D_MODEL), DEEPNORM_BETA * (A_WIDTH + B_WIDTH) ** -0.5)
        else:
            inputs[p + 'w_in'] = normal((D_MODEL, CD_IN_WIDTH), D_MODEL ** -0.5)
            inputs[p + 'conv_dw'] = normal((D_CONV_WIDTH, D_CHANNELS), D_CONV_WIDTH ** -0.5)
            inputs[p + 'conv_ln_g'] = gain((D_CHANNELS,))
            inputs[p + 'conv_ln_b'] = normal((D_CHANNELS,), 0.02)
            inputs[p + 'w_out'] = normal((C_WIDTH + D_CHANNELS, D_MODEL), DEEPNORM_BETA * (C_WIDTH + D_CHANNELS) ** -0.5)
    return inputs


def reference(x, c,
              l0_w_ada, l0_b_ada, l0_ln_g, l0_ln_b, l0_ffn1_w_up, l0_ffn1_w_down, l0_ffn2_w_up, l0_ffn2_w_down,
              l0_w_in, l0_conv_qk, l0_b_igate, l0_b_fgate, l0_mlstm_norm_g, l0_w_out,
              l1_w_ada, l1_b_ada, l1_ln_g, l1_ln_b, l1_ffn1_w_up, l1_ffn1_w_down, l1_ffn2_w_up, l1_ffn2_w_down,
              l1_w_in, l1_conv_dw, l1_conv_ln_g, l1_conv_ln_b, l1_w_out):
    common = (
        (l0_w_ada, l0_b_ada, l0_ln_g, l0_ln_b, l0_ffn1_w_up, l0_ffn1_w_down, l0_ffn2_w_up, l0_ffn2_w_down),
        (l1_w_ada, l1_b_ada, l1_ln_g, l1_ln_b, l1_ffn1_w_up, l1_ffn1_w_down, l1_ffn2_w_up, l1_ffn2_w_down),
    )
    mixers = (
        lambda u: mixer_dilated_mlstm(u, l0_w_in, l0_conv_qk, l0_b_igate, l0_b_fgate, l0_mlstm_norm_g, l0_w_out),
        lambda u: mixer_moba_conv(u, l1_w_in, l1_conv_dw, l1_conv_ln_g, l1_conv_ln_b, l1_w_out),
    )
    batch = c.shape[0]
    for layer in range(DEPTH):
        w_ada, b_ada, ln_g, ln_b, f1_up, f1_down, f2_up, f2_down = common[layer]
        mod = (jax.nn.silu(c) @ w_ada + b_ada).reshape(batch, 3 * N_SUBLAYERS, 1, D_MODEL)
        sublayers = (
            (0.5, lambda u: swiglu_ffn(u, f1_up, f1_down)),
            (1.0, mixers[layer]),
            (0.5, lambda u: swiglu_ffn(u, f2_up, f2_down)),
        )
        for j, (res_weight, fn) in enumerate(sublayers):
            shift, scale, gate = mod[:, 3 * j], mod[:, 3 * j + 1], mod[:, 3 * j + 2]
            branch = fn(x * (1.0 + scale) + shift)
            x = layer_norm(DEEPNORM_ALPHA * x + res_weight * (1.0 + gate) * branch, ln_g[j], ln_b[j])
    return x
```

```python
import functools

import jax
import jax.numpy as jnp
import numpy as np
from jax import lax
from jax.experimental import pallas as pl
from jax.experimental.pallas import tpu as pltpu

F32 = jnp.float32
BF16 = jnp.bfloat16
NEG_INF = float("-inf")

LANES = 128
SUBLANES = 8
VMEM_LIMIT_BYTES = 56 * 1024 * 1024

HEAD_DIM = 128
A_PATTERNS = ((128, 1), (512, 4), (2048, 16))
A_STEPS = 128
A_SPAN = 2048
B_HEADS = 4
B_CONV_WIDTH = 4
B_CHUNK = 256
MOBA_BLOCK = 256
MOBA_TOPK = 3
D_CONV_WIDTH = 31
N_SUBLAYERS = 3
LN_EPS = 1e-5

NT_DIMS = (((1,), (1,)), ((), ()))


def _cparams(semantics):
    return pltpu.CompilerParams(dimension_semantics=semantics, vmem_limit_bytes=VMEM_LIMIT_BYTES)


def _sigmoid(x):
    return 1.0 / (1.0 + jnp.exp(-x))


def _alibi_slopes(n_heads):
    return jnp.asarray(2.0 ** (-8.0 * np.arange(1, n_heads + 1) / n_heads), F32)


def _mm_f32_kernel(a_ref, w_ref, o_ref):
    @pl.when(pl.program_id(2) == 0)
    def _init():
        o_ref[...] = jnp.zeros_like(o_ref)

    o_ref[...] += jnp.dot(a_ref[...], w_ref[...].astype(BF16), preferred_element_type=F32)


def matmul_f32(a, w, n_out, *, tm, tn, tk):
    m, k = a.shape
    grid = (m // tm, n_out // tn, k // tk)
    return pl.pallas_call(
        _mm_f32_kernel,
        grid=grid,
        in_specs=[pl.BlockSpec((tm, tk), lambda i, j, kk: (i, kk)),
                  pl.BlockSpec((tk, tn), lambda i, j, kk: (kk, j))],
        out_specs=pl.BlockSpec((tm, tn), lambda i, j, kk: (i, j)),
        out_shape=jax.ShapeDtypeStruct((m, n_out), F32),
        compiler_params=_cparams(("parallel", "parallel", "arbitrary")),
        name="matmul_f32",
    )(a, w)


def _mm_swiglu_kernel(a_ref, wg_ref, wv_ref, o_ref, accg_ref, accv_ref):
    kk = pl.program_id(2)

    @pl.when(kk == 0)
    def _init():
        accg_ref[...] = jnp.zeros_like(accg_ref)
        accv_ref[...] = jnp.zeros_like(accv_ref)

    a = a_ref[...]
    accg_ref[...] += jnp.dot(a, wg_ref[...].astype(BF16), preferred_element_type=F32)
    accv_ref[...] += jnp.dot(a, wv_ref[...].astype(BF16), preferred_element_type=F32)

    @pl.when(kk == pl.num_programs(2) - 1)
    def _fin():
        g = accg_ref[...]
        o_ref[...] = (g * _sigmoid(g) * accv_ref[...]).astype(o_ref.dtype)


def matmul_swiglu(a, w_up, *, tm, tn, tk):
    m, k = a.shape
    f = w_up.shape[1] // 2
    nh = f // tn
    grid = (m // tm, nh, k // tk)
    return pl.pallas_call(
        _mm_swiglu_kernel,
        grid=grid,
        in_specs=[pl.BlockSpec((tm, tk), lambda i, j, kk: (i, kk)),
                  pl.BlockSpec((tk, tn), lambda i, j, kk: (kk, j)),
                  pl.BlockSpec((tk, tn), lambda i, j, kk: (kk, j + nh))],
        out_specs=pl.BlockSpec((tm, tn), lambda i, j, kk: (i, j)),
        out_shape=jax.ShapeDtypeStruct((m, f), BF16),
        scratch_shapes=[pltpu.VMEM((tm, tn), F32), pltpu.VMEM((tm, tn), F32)],
        compiler_params=_cparams(("parallel", "parallel", "arbitrary")),
        name="matmul_swiglu",
    )(a, w_up, w_up)


def _ada_kernel(c_ref, w_ref, b_ref, o_ref):
    c = c_ref[...]
    a = (c * _sigmoid(c)).astype(BF16)
    o_ref[...] = jnp.dot(a, w_ref[...].astype(BF16), preferred_element_type=F32) + b_ref[...]


def ada_modulation(c_pad, w_ada, b_ada, *, tn):
    rows, d = c_pad.shape
    n = w_ada.shape[1]
    return pl.pallas_call(
        _ada_kernel,
        grid=(n // tn,),
        in_specs=[pl.BlockSpec((rows, d), lambda j: (0, 0)),
                  pl.BlockSpec((d, tn), lambda j: (0, j)),
                  pl.BlockSpec((1, tn), lambda j: (0, j))],
        out_specs=pl.BlockSpec((rows, tn), lambda j: (0, j)),
        out_shape=jax.ShapeDtypeStruct((rows, n), F32),
        compiler_params=_cparams(("parallel",)),
        name="ada_modulation",
    )(c_pad, w_ada, b_ada.reshape(1, n))


def _modulate_kernel(x_ref, sc_ref, sh_ref, u_ref):
    u_ref[...] = (x_ref[...] * (1.0 + sc_ref[...]) + sh_ref[...]).astype(u_ref.dtype)


def modulate(x2, mod, j_shift, j_scale, *, rows_per_batch, tm):
    m, d = x2.shape
    bpb = rows_per_batch // tm

    def mspec(j):
        return pl.BlockSpec((None, None, 1, d), lambda i: (i // bpb, j, 0, 0))

    return pl.pallas_call(
        _modulate_kernel,
        grid=(m // tm,),
        in_specs=[pl.BlockSpec((tm, d), lambda i: (i, 0)), mspec(j_scale), mspec(j_shift)],
        out_specs=pl.BlockSpec((tm, d), lambda i: (i, 0)),
        out_shape=jax.ShapeDtypeStruct((m, d), BF16),
        compiler_params=_cparams(("parallel",)),
        name="modulate",
    )(x2, mod, mod)


def _ln_rows(z, g, b):
    mu = jnp.mean(z, axis=-1, keepdims=True)
    zc = z - mu
    var = jnp.mean(zc * zc, axis=-1, keepdims=True)
    return zc * lax.rsqrt(var + LN_EPS) * g + b


def _ln_res_kernel(x_ref, y_ref, gate_ref, g_ref, b_ref, *rest, alpha, res_w, tm, with_u):
    if with_u:
        sc_ref, sh_ref, xo_ref, uo_ref = rest
    else:
        (xo_ref,) = rest
    gate = res_w * (1.0 + gate_ref[...])
    g = g_ref[...]
    b = b_ref[...]

    def rows(r, carry):
        sl = pl.ds(pl.multiple_of(r * SUBLANES, SUBLANES), SUBLANES)
        z = alpha * x_ref[sl, :] + gate * y_ref[sl, :]
        xn = _ln_rows(z, g, b)
        xo_ref[sl, :] = xn
        if with_u:
            uo_ref[sl, :] = (xn * (1.0 + sc_ref[...]) + sh_ref[...]).astype(uo_ref.dtype)
        return carry

    lax.fori_loop(0, tm // SUBLANES, rows, 0)


def ln_residual(x2, y2, mod, j_gate, ln_g, ln_b, j_ln, res_w, alpha, next_mod, next_j, *, rows_per_batch, tm):
    m, d = x2.shape
    bpb = rows_per_batch // tm
    with_u = next_mod is not None

    def mspec(j):
        return pl.BlockSpec((None, None, 1, d), lambda i: (i // bpb, j, 0, 0))

    row_spec = pl.BlockSpec((tm, d), lambda i: (i, 0))
    ln_spec = pl.BlockSpec((None, 1, d), lambda i: (j_ln, 0, 0))
    in_specs = [row_spec, row_spec, mspec(j_gate), ln_spec, ln_spec]
    args = [x2, y2, mod, ln_g.reshape(N_SUBLAYERS, 1, d), ln_b.reshape(N_SUBLAYERS, 1, d)]
    out_specs = [row_spec]
    out_shape = [jax.ShapeDtypeStruct((m, d), F32)]
    if with_u:
        j_shift, j_scale = next_j
        in_specs += [mspec(j_scale), mspec(j_shift)]
        args += [next_mod, next_mod]
        out_specs.append(row_spec)
        out_shape.append(jax.ShapeDtypeStruct((m, d), BF16))
    outs = pl.pallas_call(
        functools.partial(_ln_res_kernel, alpha=alpha, res_w=res_w, tm=tm, with_u=with_u),
        grid=(m // tm,),
        in_specs=in_specs,
        out_specs=out_specs,
        out_shape=out_shape,
        compiler_params=_cparams(("parallel",)),
        name="ln_residual",
    )(*args)
    return (outs[0], outs[1]) if with_u else (outs[0], None)


def _dilated_kernel(slopes_ref, q_ref, kc_ref, kp_ref, vc_ref, vp_ref, o_ref,
                    kk_ref, vv_ref, *acc_refs, span, steps, patterns, scale):
    h = pl.program_id(1)
    i = pl.program_id(2)
    slope = slopes_ref[h]
    n_pat = len(patterns)
    o_acc = acc_refs[:n_pat]
    m_acc = acc_refs[n_pat:2 * n_pat]
    l_acc = acc_refs[2 * n_pat:]

    kk_ref[0:span, :] = kp_ref[...]
    kk_ref[span:2 * span, :] = kc_ref[...]
    vv_ref[0:span, :] = vp_ref[...]
    vv_ref[span:2 * span, :] = vc_ref[...]

    qi = lax.broadcasted_iota(jnp.int32, (steps, 2 * steps), 0)
    kj = lax.broadcasted_iota(jnp.int32, (steps, 2 * steps), 1)
    back = steps + qi - kj
    in_band = (back >= 0) & (back <= steps)
    backf = back.astype(F32)

    for g, (_, dil) in enumerate(patterns):
        def tile(t, carry, g=g, dil=dil):
            r = t % dil
            n = t // dil
            q_start = r + dil * steps * n
            k_start = span + q_start - dil * steps
            q = q_ref[pl.ds(q_start, steps, stride=dil), :] * scale
            k = kk_ref[pl.ds(k_start, 2 * steps, stride=dil), :]
            v = vv_ref[pl.ds(k_start, 2 * steps, stride=dil), :]
            s = lax.dot_general(q.astype(BF16), k.astype(BF16), NT_DIMS, preferred_element_type=F32)
            first_key = jnp.where((n + i) > 0, 0, steps)
            valid = in_band & (kj >= first_key)
            s = jnp.where(valid, s - (slope * dil) * backf, NEG_INF)
            mx = jnp.max(s, axis=1, keepdims=True)
            p = jnp.exp(s - mx)
            o_acc[g][pl.ds(q_start, steps, stride=dil), :] = jnp.dot(
                p.astype(BF16), v.astype(BF16), preferred_element_type=F32)
            m_acc[g][pl.ds(q_start, steps, stride=dil), :] = mx
            l_acc[g][pl.ds(q_start, steps, stride=dil), :] = jnp.sum(p, axis=1, keepdims=True)
            return carry

        lax.fori_loop(0, span // steps, tile, 0)

    def merge(c, carry):
        sl = pl.ds(pl.multiple_of(c * steps, steps), steps)
        ms = [m[sl, :] for m in m_acc]
        top = functools.reduce(jnp.maximum, ms)
        num = jnp.zeros((steps, o_ref.shape[-1]), F32)
        den = jnp.zeros((steps, 1), F32)
        for g in range(n_pat):
            w = jnp.exp(ms[g] - top)
            num = num + w * o_acc[g][sl, :]
            den = den + w * l_acc[g][sl, :]
        o_ref[sl, :] = (num / den).astype(o_ref.dtype)
        return carry

    lax.fori_loop(0, span // steps, merge, 0)


def dilated_attention(qkv, n_heads, *, q_col, k_col, v_col):
    b, s, _ = qkv.shape
    dh = HEAD_DIM
    span, steps = A_SPAN, A_STEPS
    n_pat = len(A_PATTERNS)

    def cur(col):
        return pl.BlockSpec((None, span, dh), lambda bb, h, i: (bb, i, col + h))

    def prev(col):
        return pl.BlockSpec((None, span, dh), lambda bb, h, i: (bb, jnp.maximum(i - 1, 0), col + h))

    scratch = [pltpu.VMEM((2 * span, dh), F32), pltpu.VMEM((2 * span, dh), F32)]
    scratch += [pltpu.VMEM((span, dh), F32) for _ in range(n_pat)]
    scratch += [pltpu.VMEM((span, 1), F32) for _ in range(2 * n_pat)]
    return pl.pallas_call(
        functools.partial(_dilated_kernel, span=span, steps=steps, patterns=A_PATTERNS, scale=dh ** -0.5),
        grid=(b, n_heads, s // span),
        in_specs=[pl.BlockSpec(memory_space=pltpu.SMEM),
                  cur(q_col), cur(k_col), prev(k_col), cur(v_col), prev(v_col)],
        out_specs=pl.BlockSpec((None, span, dh), lambda bb, h, i: (bb, i, h)),
        out_shape=jax.ShapeDtypeStruct((b, s, n_heads * dh), BF16),
        scratch_shapes=scratch,
        compiler_params=_cparams(("parallel", "parallel", "arbitrary")),
        name="dilated_attention",
    )(_alibi_slopes(n_heads), qkv, qkv, qkv, qkv, qkv)


def _conv4_kernel(x_ref, halo_ref, w_ref, o_ref, buf_ref, *, t_rows, width, k_blocks_from, k_scale):
    i = pl.program_id(1)
    j = pl.program_id(2)
    halo = jnp.where(i > 0, halo_ref[...], 0.0)
    buf_ref[0:SUBLANES, :] = halo
    buf_ref[SUBLANES:SUBLANES + t_rows, :] = x_ref[...]
    off = SUBLANES - (width - 1)
    acc = jnp.zeros(x_ref.shape, F32)
    for tap in range(width):
        acc = acc + w_ref[tap:tap + 1, :] * buf_ref[off + tap:off + tap + t_rows, :]
    y = acc * _sigmoid(acc)
    y = y * jnp.where(j >= k_blocks_from, k_scale, 1.0)
    o_ref[...] = y.astype(o_ref.dtype)


def conv4_silu(qkv, conv_w, *, col0, width_cols, k_scale, t_rows, tc):
    b, s, _ = qkv.shape
    width = conv_w.shape[0]
    cb0 = col0 // tc
    n_cb = width_cols // tc
    rb = t_rows // SUBLANES
    return pl.pallas_call(
        functools.partial(_conv4_kernel, t_rows=t_rows, width=width, k_blocks_from=n_cb // 2, k_scale=k_scale),
        grid=(b, s // t_rows, n_cb),
        in_specs=[pl.BlockSpec((None, t_rows, tc), lambda bb, i, j: (bb, i, cb0 + j)),
                  pl.BlockSpec((None, SUBLANES, tc), lambda bb, i, j: (bb, jnp.maximum(i * rb - 1, 0), cb0 + j)),
                  pl.BlockSpec((width, tc), lambda bb, i, j: (0, j))],
        out_specs=pl.BlockSpec((None, t_rows, tc), lambda bb, i, j: (bb, i, j)),
        out_shape=jax.ShapeDtypeStruct((b, s, width_cols), BF16),
        scratch_shapes=[pltpu.VMEM((SUBLANES + t_rows, tc), F32)],
        compiler_params=_cparams(("parallel", "parallel", "parallel")),
        name="conv4_silu",
    )(qkv, qkv, conv_w)


def _mlstm_kernel(big_ref, bfg_ref, gates_ref, q_ref, k_ref, v_ref, og_ref, ng_ref, o_ref,
                  c_ref, n_ref, m_ref, *, chunk):
    h = pl.program_id(1)
    c = pl.program_id(2)

    @pl.when(c == 0)
    def _init():
        c_ref[...] = jnp.zeros_like(c_ref)
        n_ref[...] = jnp.zeros_like(n_ref)
        m_ref[...] = jnp.zeros_like(m_ref)

    gates = gates_ref[...]
    i_row = gates[0:1, :] + big_ref[h]
    f_row = gates[1:2, :] + bfg_ref[h]
    lf_row = jnp.minimum(f_row, 0.0) - jnp.log(1.0 + jnp.exp(-jnp.abs(f_row)))

    tt = lax.broadcasted_iota(jnp.int32, (chunk, chunk), 0)
    ss = lax.broadcasted_iota(jnp.int32, (chunk, chunk), 1)
    causal = ss <= tt
    diag = ss == tt
    lf_b = jnp.broadcast_to(lf_row, (chunk, chunk))
    cum_col = jnp.sum(jnp.where(causal, lf_b, 0.0), axis=1, keepdims=True)
    lf_col = jnp.sum(jnp.where(diag, lf_b, 0.0), axis=1, keepdims=True)
    i_col = jnp.sum(jnp.where(diag, jnp.broadcast_to(i_row, (chunk, chunk)), 0.0), axis=1, keepdims=True)
    cum_row = jnp.sum(jnp.where(tt <= ss, jnp.broadcast_to(lf_col, (chunk, chunk)), 0.0),
                      axis=0, keepdims=True)
    cum_last = jnp.sum(lf_row, axis=1, keepdims=True)
    m_prev = m_ref[...]

    log_w = jnp.where(causal, cum_col - cum_row + i_row, NEG_INF)
    log_inter = cum_col + m_prev
    m_row = jnp.maximum(log_inter, jnp.max(log_w, axis=1, keepdims=True))
    w_intra = jnp.exp(log_w - m_row)
    w_inter = jnp.exp(log_inter - m_row)

    q = q_ref[...]
    k = k_ref[...]
    v = v_ref[...].astype(BF16)
    c_state = c_ref[...]
    n_state = n_ref[...]
    attn = w_intra * lax.dot_general(q, k, NT_DIMS, preferred_element_type=F32)
    num = (w_inter * jnp.dot(q, c_state.astype(BF16), preferred_element_type=F32)
           + jnp.dot(attn.astype(BF16), v, preferred_element_type=F32))
    den = (w_inter * jnp.sum(q.astype(F32) * n_state, axis=1, keepdims=True)
           + jnp.sum(attn, axis=1, keepdims=True))
    h_out = num / jnp.maximum(jnp.abs(den), jnp.exp(-m_row))

    lte_col = cum_last - cum_col + i_col
    lte_row = cum_last - cum_row + i_row
    m_new = jnp.maximum(cum_last + m_prev, jnp.max(lte_row, axis=1, keepdims=True))
    w_end = jnp.exp(lte_col - m_new)
    decay = jnp.exp(cum_last + m_prev - m_new)
    kw = k.astype(F32) * w_end
    c_ref[...] = decay * c_state + jnp.dot(kw.T.astype(BF16), v, preferred_element_type=F32)
    n_ref[...] = decay * n_state + jnp.sum(kw, axis=0, keepdims=True)
    m_ref[...] = m_new

    mu = jnp.mean(h_out, axis=1, keepdims=True)
    hc = h_out - mu
    var = jnp.mean(hc * hc, axis=1, keepdims=True)
    h_norm = hc * lax.rsqrt(var + LN_EPS) * ng_ref[...]
    o_ref[...] = (_sigmoid(og_ref[...]) * h_norm).astype(o_ref.dtype)


def mlstm(qkv, qk_conv, gates, b_igate, b_fgate, norm_g, *, v_col0, o_col0):
    b, s, _ = qkv.shape
    n_heads = gates.shape[1]
    dk = qk_conv.shape[-1] // (2 * n_heads)
    dv = norm_g.shape[-1] // n_heads
    chunk = B_CHUNK
    return pl.pallas_call(
        functools.partial(_mlstm_kernel, chunk=chunk),
        grid=(b, n_heads, s // chunk),
        in_specs=[pl.BlockSpec(memory_space=pltpu.SMEM),
                  pl.BlockSpec(memory_space=pltpu.SMEM),
                  pl.BlockSpec((None, None, 2, chunk), lambda bb, h, c: (bb, h, 0, c)),
                  pl.BlockSpec((None, chunk, dk), lambda bb, h, c: (bb, c, h)),
                  pl.BlockSpec((None, chunk, dk), lambda bb, h, c: (bb, c, n_heads + h)),
                  pl.BlockSpec((None, chunk, dv), lambda bb, h, c: (bb, c, v_col0 // dv + h)),
                  pl.BlockSpec((None, chunk, dv), lambda bb, h, c: (bb, c, o_col0 // dv + h)),
                  pl.BlockSpec((1, dv), lambda bb, h, c: (0, h))],
        out_specs=pl.BlockSpec((None, chunk, dv), lambda bb, h, c: (bb, c, h)),
        out_shape=jax.ShapeDtypeStruct((b, s, n_heads * dv), BF16),
        scratch_shapes=[pltpu.VMEM((dk, dv), F32), pltpu.VMEM((1, dk), F32), pltpu.VMEM((1, 1), F32)],
        compiler_params=_cparams(("parallel", "parallel", "arbitrary")),
        name="mlstm",
    )(b_igate, b_fgate, gates, qk_conv, qk_conv, qkv, qkv, norm_g.reshape(1, -1))


def _moba_kernel(slopes_ref, q_ref, k_ref, v_ref, o_ref, kmean_ref, sel_ref, m_ref, l_ref, acc_ref,
                 *, blk, n_blk, top_k, scale):
    h = pl.program_id(1)
    qb = pl.program_id(2)
    slope = slopes_ref[h]

    @pl.when(qb == 0)
    def _block_means():
        kmean_ref[...] = jnp.zeros_like(kmean_ref)
        for n in range(n_blk):
            kmean_ref[n:n + 1, :] = jnp.mean(k_ref[n * blk:(n + 1) * blk, :], axis=0, keepdims=True)

    q = q_ref[...] * scale
    gate = lax.dot_general(q, kmean_ref[...], NT_DIMS, precision=lax.Precision.HIGHEST,
                           preferred_element_type=F32)
    lane = lax.broadcasted_iota(jnp.int32, gate.shape, 1)
    gate = jnp.where(lane < qb, gate, NEG_INF)
    sel = jnp.zeros(gate.shape, F32)
    for j in range(top_k):
        mx = jnp.max(gate, axis=1, keepdims=True)
        idx = jnp.min(jnp.where(gate == mx, lane, LANES), axis=1, keepdims=True)
        pick = lane == idx
        sel = sel + jnp.where(pick, jnp.where(qb > j, 1.0, 0.0), 0.0)
        gate = jnp.where(pick, NEG_INF, gate)
    sel_ref[...] = sel

    qbf = q.astype(BF16)
    ri = lax.broadcasted_iota(jnp.int32, (blk, blk), 0)
    cj = lax.broadcasted_iota(jnp.int32, (blk, blk), 1)
    rel_i = ri - cj
    rel = rel_i.astype(F32)

    own = pl.ds(pl.multiple_of(qb * blk, blk), blk)
    s = lax.dot_general(qbf, k_ref[own, :].astype(BF16), NT_DIMS, preferred_element_type=F32)
    s = jnp.where(ri >= cj, s - slope * rel, NEG_INF)
    mx = jnp.max(s, axis=1, keepdims=True)
    p = jnp.exp(s - mx)
    m_ref[...] = mx
    l_ref[...] = jnp.sum(p, axis=1, keepdims=True)
    acc_ref[...] = jnp.dot(p.astype(BF16), v_ref[own, :].astype(BF16), preferred_element_type=F32)

    def past(n, carry):
        rows = pl.ds(pl.multiple_of(n * blk, blk), blk)
        s = lax.dot_general(qbf, k_ref[rows, :].astype(BF16), NT_DIMS, preferred_element_type=F32)
        chosen = jnp.max(jnp.where(lane == n, sel_ref[...], 0.0), axis=1, keepdims=True) > 0.0
        dist = (rel_i + (qb - n) * blk).astype(F32)
        s = jnp.where(chosen, s - slope * dist, NEG_INF)
        m_old = m_ref[...]
        m_new = jnp.maximum(m_old, jnp.max(s, axis=1, keepdims=True))
        a = jnp.exp(m_old - m_new)
        p = jnp.exp(s - m_new)
        l_ref[...] = a * l_ref[...] + jnp.sum(p, axis=1, keepdims=True)
        acc_ref[...] = a * acc_ref[...] + jnp.dot(p.astype(BF16), v_ref[rows, :].astype(BF16),
                                                  preferred_element_type=F32)
        m_ref[...] = m_new
        return carry

    lax.fori_loop(0, qb, past, 0)
    o_ref[...] = (acc_ref[...] / l_ref[...]).astype(o_ref.dtype)


def moba_attention(qkv, n_heads, *, q_col, k_col, v_col):
    b, s, _ = qkv.shape
    dh = HEAD_DIM
    blk = MOBA_BLOCK
    n_blk = s // blk
    top_k = min(MOBA_TOPK, n_blk)
    assert n_blk <= LANES
    return pl.pallas_call(
        functools.partial(_moba_kernel, blk=blk, n_blk=n_blk, top_k=top_k, scale=dh ** -0.5),
        grid=(b, n_heads, n_blk),
        in_specs=[pl.BlockSpec(memory_space=pltpu.SMEM),
                  pl.BlockSpec((None, blk, dh), lambda bb, h, i: (bb, i, q_col + h)),
                  pl.BlockSpec((None, s, dh), lambda bb, h, i: (bb, 0, k_col + h)),
                  pl.BlockSpec((None, s, dh), lambda bb, h, i: (bb, 0, v_col + h))],
        out_specs=pl.BlockSpec((None, blk, dh), lambda bb, h, i: (bb, i, h)),
        out_shape=jax.ShapeDtypeStruct((b, s, n_heads * dh), BF16),
        scratch_shapes=[pltpu.VMEM((LANES, dh), F32), pltpu.VMEM((blk, LANES), F32),
                        pltpu.VMEM((blk, 1), F32), pltpu.VMEM((blk, 1), F32), pltpu.VMEM((blk, dh), F32)],
        compiler_params=_cparams(("parallel", "parallel", "arbitrary")),
        name="moba_attention",
    )(_alibi_slopes(n_heads), qkv, qkv, qkv)


CONV_HALO = 32
LN_ROWS = 16


def _conv31_kernel(a_ref, ah_ref, b_ref, bh_ref, w_ref, g_ref, beta_ref, o_ref, glu_ref, conv_ref,
                   *, t_rows, width, channels):
    i = pl.program_id(1)
    halo = ah_ref[...] * _sigmoid(bh_ref[...])
    glu_ref[0:CONV_HALO, :] = jnp.where(i > 0, halo, 0.0)
    glu_ref[CONV_HALO:CONV_HALO + t_rows, :] = a_ref[...] * _sigmoid(b_ref[...])
    off = CONV_HALO - (width - 1)

    def cols(c, carry):
        cs = pl.ds(pl.multiple_of(c * LANES, LANES), LANES)
        acc = jnp.zeros((t_rows, LANES), F32)
        for tap in range(width):
            acc = acc + w_ref[tap:tap + 1, cs] * glu_ref[off + tap:off + tap + t_rows, cs]
        conv_ref[:, cs] = acc
        return carry

    lax.fori_loop(0, channels // LANES, cols, 0)

    def rows(r, carry):
        rs = pl.ds(pl.multiple_of(r * LN_ROWS, LN_ROWS), LN_ROWS)
        y = _ln_rows(conv_ref[rs, :], g_ref[...], beta_ref[...])
        o_ref[rs, :] = (y * _sigmoid(y)).astype(o_ref.dtype)
        return carry

    lax.fori_loop(0, t_rows // LN_ROWS, rows, 0)


def glu_conv_ln_silu(qkv, conv_w, ln_g, ln_b, *, a_col0, b_col0, channels, t_rows):
    b, s, _ = qkv.shape
    width = conv_w.shape[0]
    hb = t_rows // CONV_HALO
    ca, cb = a_col0 // channels, b_col0 // channels

    def cur(col):
        return pl.BlockSpec((None, t_rows, channels), lambda bb, i: (bb, i, col))

    def halo(col):
        return pl.BlockSpec((None, CONV_HALO, channels), lambda bb, i: (bb, jnp.maximum(i * hb - 1, 0), col))

    vec = pl.BlockSpec((1, channels), lambda bb, i: (0, 0))
    return pl.pallas_call(
        functools.partial(_conv31_kernel, t_rows=t_rows, width=width, channels=channels),
        grid=(b, s // t_rows),
        in_specs=[cur(ca), halo(ca), cur(cb), halo(cb),
                  pl.BlockSpec((width, channels), lambda bb, i: (0, 0)), vec, vec],
        out_specs=pl.BlockSpec((None, t_rows, channels), lambda bb, i: (bb, i, 0)),
        out_shape=jax.ShapeDtypeStruct((b, s, channels), BF16),
        scratch_shapes=[pltpu.VMEM((CONV_HALO + t_rows, channels), F32), pltpu.VMEM((t_rows, channels), F32)],
        compiler_params=_cparams(("parallel", "parallel")),
        name="glu_conv_ln_silu",
    )(qkv, qkv, qkv, qkv, conv_w, ln_g.reshape(1, -1), ln_b.reshape(1, -1))


MM_TILES = dict(tm=2048, tn=1024, tk=1024)
SWIGLU_TILES = dict(tm=2048, tn=512, tk=1024)
ROW_TILE = 256


def _ffn(u, w_up, w_down):
    hmid = matmul_swiglu(u, w_up, **SWIGLU_TILES)
    return matmul_f32(hmid, w_down, w_down.shape[1], **MM_TILES)


def _mixer_dilated_mlstm(u, bsz, seq, w_in, conv_qk, b_igate, b_fgate, norm_g, w_out):
    d = u.shape[1]
    a_width = d // 2
    a_heads = a_width // HEAD_DIM
    b_width = d - a_width
    qk_width = b_width // 2
    n_main = 3 * a_width + 2 * qk_width + 2 * b_width
    proj = matmul_f32(u, w_in, n_main, **MM_TILES).reshape(bsz, seq, n_main)
    w_gate = jnp.pad(w_in[:, n_main:], ((0, 0), (0, LANES - 2 * B_HEADS)))
    gates = matmul_f32(u, w_gate, LANES, tm=MM_TILES["tm"], tn=LANES, tk=MM_TILES["tk"])
    gates = gates[:, :2 * B_HEADS].reshape(bsz, seq, 2, B_HEADS).transpose(0, 3, 2, 1)

    hb = a_width // HEAD_DIM
    a_out = dilated_attention(proj, a_heads, q_col=0, k_col=hb, v_col=2 * hb)
    qk_col0 = 3 * a_width
    qk_conv = conv4_silu(proj, conv_qk, col0=qk_col0, width_cols=2 * qk_width,
                         k_scale=(qk_width // B_HEADS) ** -0.5, t_rows=512, tc=512)
    b_out = mlstm(proj, qk_conv, gates, b_igate, b_fgate, norm_g,
                  v_col0=qk_col0 + 2 * qk_width, o_col0=qk_col0 + 2 * qk_width + b_width)
    cat = jnp.concatenate([a_out, b_out], axis=-1).reshape(bsz * seq, d)
    return matmul_f32(cat, w_out, d, **MM_TILES)


def _mixer_moba_conv(u, bsz, seq, w_in, conv_dw, conv_ln_g, conv_ln_b, w_out):
    d = u.shape[1]
    c_width = d // 2
    c_heads = c_width // HEAD_DIM
    d_ch = d - c_width
    n_main = 3 * c_width + 2 * d_ch
    proj = matmul_f32(u, w_in, n_main, **MM_TILES).reshape(bsz, seq, n_main)
    c_out = moba_attention(proj, c_heads, q_col=0, k_col=c_heads, v_col=2 * c_heads)
    d_out = glu_conv_ln_silu(proj, conv_dw, conv_ln_g, conv_ln_b, a_col0=3 * c_width,
                             b_col0=3 * c_width + d_ch, channels=d_ch, t_rows=ROW_TILE)
    cat = jnp.concatenate([c_out, d_out], axis=-1).reshape(bsz * seq, d)
    return matmul_f32(cat, w_out, d, **MM_TILES)


def kernel(x, c, l0_w_ada, l0_b_ada, l0_ln_g, l0_ln_b, l0_ffn1_w_up, l0_ffn1_w_down, l0_ffn2_w_up, l0_ffn2_w_down, l0_w_in, l0_conv_qk, l0_b_igate, l0_b_fgate, l0_mlstm_norm_g, l0_w_out, l1_w_ada, l1_b_ada, l1_ln_g, l1_ln_b, l1_ffn1_w_up, l1_ffn1_w_down, l1_ffn2_w_up, l1_ffn2_w_down, l1_w_in, l1_conv_dw, l1_conv_ln_g, l1_conv_ln_b, l1_w_out):
    bsz, seq, d = x.shape
    depth = 2
    alpha = (2 * depth) ** 0.25
    layers = (
        (l0_w_ada, l0_b_ada, l0_ln_g, l0_ln_b, l0_ffn1_w_up, l0_ffn1_w_down, l0_ffn2_w_up, l0_ffn2_w_down),
        (l1_w_ada, l1_b_ada, l1_ln_g, l1_ln_b, l1_ffn1_w_up, l1_ffn1_w_down, l1_ffn2_w_up, l1_ffn2_w_down),
    )
    mixers = (
        lambda u: _mixer_dilated_mlstm(u, bsz, seq, l0_w_in, l0_conv_qk, l0_b_igate, l0_b_fgate,
                                       l0_mlstm_norm_g, l0_w_out),
        lambda u: _mixer_moba_conv(u, bsz, seq, l1_w_in, l1_conv_dw, l1_conv_ln_g, l1_conv_ln_b, l1_w_out),
    )
    c_pad = jnp.pad(c, ((0, SUBLANES - bsz), (0, 0)))
    mods = [ada_modulation(c_pad, p[0], p[1], tn=512)[:bsz].reshape(bsz, 3 * N_SUBLAYERS, 1, d) for p in layers]

    x2 = x.reshape(bsz * seq, d)
    u = modulate(x2, mods[0], 0, 1, rows_per_batch=seq, tm=ROW_TILE)
    for layer in range(depth):
        _, _, ln_g, ln_b, f1_up, f1_down, f2_up, f2_down = layers[layer]
        branches = (
            (0.5, lambda v: _ffn(v, f1_up, f1_down)),
            (1.0, mixers[layer]),
            (0.5, lambda v: _ffn(v, f2_up, f2_down)),
        )
        for j, (res_w, fn) in enumerate(branches):
            y = fn(u)
            if j + 1 < N_SUBLAYERS:
                nxt, nj = mods[layer], (3 * (j + 1), 3 * (j + 1) + 1)
            elif layer + 1 < depth:
                nxt, nj = mods[layer + 1], (0, 1)
            else:
                nxt, nj = None, None
            x2, u = ln_residual(x2, y, mods[layer], 3 * j + 2, ln_g, ln_b, j, res_w, alpha, nxt, nj,
                                rows_per_batch=seq, tm=ROW_TILE)
    return x2.reshape(bsz, seq, d)
```

```python
import functools

import jax
import jax.numpy as jnp
import numpy as np
from jax import lax
from jax.experimental import pallas as pl
from jax.experimental.pallas import tpu as pltpu

F32 = jnp.float32
BF16 = jnp.bfloat16
NEG_INF = float("-inf")

LANES = 128
SUBLANES = 8
VMEM_LIMIT_BYTES = 56 * 1024 * 1024

HEAD_DIM = 128
A_PATTERNS = ((128, 1), (512, 4), (2048, 16))
A_STEPS = 128
A_SPAN = 2048
B_HEADS = 4
B_CONV_WIDTH = 4
B_CHUNK = 256
MOBA_BLOCK = 256
MOBA_TOPK = 3
D_CONV_WIDTH = 31
N_SUBLAYERS = 3
LN_EPS = 1e-5

NT_DIMS = (((1,), (1,)), ((), ()))


def _cparams(semantics):
    return pltpu.CompilerParams(dimension_semantics=semantics, vmem_limit_bytes=VMEM_LIMIT_BYTES)


def _sigmoid(x):
    return 1.0 / (1.0 + jnp.exp(-x))


def _alibi_slopes(n_heads):
    return jnp.asarray(2.0 ** (-8.0 * np.arange(1, n_heads + 1) / n_heads), F32)


def _mm_f32_kernel(a_ref, w_ref, o_ref):
    @pl.when(pl.program_id(2) == 0)
    def _init():
        o_ref[...] = jnp.zeros_like(o_ref)

    o_ref[...] += jnp.dot(a_ref[...], w_ref[...].astype(BF16), preferred_element_type=F32)


def matmul_f32(a, w, n_out, *, tm, tn, tk):
    m, k = a.shape
    grid = (m // tm, n_out // tn, k // tk)
    return pl.pallas_call(
        _mm_f32_kernel,
        grid=grid,
        in_specs=[pl.BlockSpec((tm, tk), lambda i, j, kk: (i, kk)),
                  pl.BlockSpec((tk, tn), lambda i, j, kk: (kk, j))],
        out_specs=pl.BlockSpec((tm, tn), lambda i, j, kk: (i, j)),
        out_shape=jax.ShapeDtypeStruct((m, n_out), F32),
        compiler_params=_cparams(("parallel", "parallel", "arbitrary")),
        name="matmul_f32",
    )(a, w)


def _mm_rows_kernel(*refs, n_a):
    a_refs, w_ref, o_ref = refs[:n_a], refs[n_a], refs[n_a + 1]
    acc, off = None, 0
    for a_ref in a_refs:
        ka = a_ref.shape[1]
        part = jnp.dot(a_ref[...], w_ref[off:off + ka, :].astype(BF16), preferred_element_type=F32)
        acc = part if acc is None else acc + part
        off += ka
    o_ref[...] = acc.astype(o_ref.dtype)


def matmul_rows(a_list, w, n_out, *, tm, tn, col_block0=0, single_buffer_a=False):
    m = a_list[0].shape[0]
    k = w.shape[0]
    assert sum(a.shape[1] for a in a_list) == k
    mode = dict(pipeline_mode=pl.Buffered(1)) if single_buffer_a else {}
    in_specs = [pl.BlockSpec((tm, a.shape[1]), lambda i, j: (i, 0), **mode) for a in a_list]
    in_specs.append(pl.BlockSpec((k, tn), lambda i, j: (0, j + col_block0)))
    return pl.pallas_call(
        functools.partial(_mm_rows_kernel, n_a=len(a_list)),
        grid=(m // tm, pl.cdiv(n_out, tn)),
        in_specs=in_specs,
        out_specs=pl.BlockSpec((tm, tn), lambda i, j: (i, j)),
        out_shape=jax.ShapeDtypeStruct((m, n_out), F32),
        compiler_params=_cparams(("parallel", "arbitrary")),
        name="matmul_rows",
    )(*a_list, w)


def _mm_swiglu_kernel(a_ref, wg_ref, wv_ref, o_ref):
    a = a_ref[...]
    g = jnp.dot(a, wg_ref[...].astype(BF16), preferred_element_type=F32)
    v = jnp.dot(a, wv_ref[...].astype(BF16), preferred_element_type=F32)
    o_ref[...] = (g * _sigmoid(g) * v).astype(o_ref.dtype)


def matmul_swiglu(a, w_up, *, tm, tn):
    m, k = a.shape
    f = w_up.shape[1] // 2
    nh = f // tn
    return pl.pallas_call(
        _mm_swiglu_kernel,
        grid=(m // tm, nh),
        in_specs=[pl.BlockSpec((tm, k), lambda i, j: (i, 0), pipeline_mode=pl.Buffered(1)),
                  pl.BlockSpec((k, tn), lambda i, j: (0, j)),
                  pl.BlockSpec((k, tn), lambda i, j: (0, j + nh))],
        out_specs=pl.BlockSpec((tm, tn), lambda i, j: (i, j)),
        out_shape=jax.ShapeDtypeStruct((m, f), BF16),
        compiler_params=_cparams(("parallel", "arbitrary")),
        name="matmul_swiglu",
    )(a, w_up, w_up)


def _ada_kernel(c_ref, w_ref, b_ref, o_ref):
    c = c_ref[...]
    a = (c * _sigmoid(c)).astype(BF16)
    o_ref[...] = jnp.dot(a, w_ref[...].astype(BF16), preferred_element_type=F32) + b_ref[...]


def ada_modulation(c_pad, w_ada, b_ada, *, tn):
    rows, d = c_pad.shape
    n = w_ada.shape[1]
    return pl.pallas_call(
        _ada_kernel,
        grid=(n // tn,),
        in_specs=[pl.BlockSpec((rows, d), lambda j: (0, 0)),
                  pl.BlockSpec((d, tn), lambda j: (0, j)),
                  pl.BlockSpec((1, tn), lambda j: (0, j))],
        out_specs=pl.BlockSpec((rows, tn), lambda j: (0, j)),
        out_shape=jax.ShapeDtypeStruct((rows, n), F32),
        compiler_params=_cparams(("parallel",)),
        name="ada_modulation",
    )(c_pad, w_ada, b_ada.reshape(1, n))


def _modulate_kernel(x_ref, sc_ref, sh_ref, u_ref):
    u_ref[...] = (x_ref[...] * (1.0 + sc_ref[...]) + sh_ref[...]).astype(u_ref.dtype)


def modulate(x2, mod, j_shift, j_scale, *, rows_per_batch, tm):
    m, d = x2.shape
    bpb = rows_per_batch // tm

    def mspec(j):
        return pl.BlockSpec((None, None, 1, d), lambda i: (i // bpb, j, 0, 0))

    return pl.pallas_call(
        _modulate_kernel,
        grid=(m // tm,),
        in_specs=[pl.BlockSpec((tm, d), lambda i: (i, 0)), mspec(j_scale), mspec(j_shift)],
        out_specs=pl.BlockSpec((tm, d), lambda i: (i, 0)),
        out_shape=jax.ShapeDtypeStruct((m, d), BF16),
        compiler_params=_cparams(("parallel",)),
        name="modulate",
    )(x2, mod, mod)


def _ln_rows(z, g, b):
    mu = jnp.mean(z, axis=-1, keepdims=True)
    zc = z - mu
    var = jnp.mean(zc * zc, axis=-1, keepdims=True)
    return zc * lax.rsqrt(var + LN_EPS) * g + b


def _ln_res_kernel(x_ref, y_ref, gate_ref, g_ref, b_ref, *rest, alpha, res_w, tm, with_u):
    if with_u:
        sc_ref, sh_ref, xo_ref, uo_ref, gw_ref, g2_ref, b2_ref = rest
        g2_ref[...] = g_ref[...] * (1.0 + sc_ref[...])
        b2_ref[...] = b_ref[...] * (1.0 + sc_ref[...]) + sh_ref[...]
    else:
        xo_ref, gw_ref = rest
    gw_ref[...] = res_w * (1.0 + gate_ref[...])
    inv_d = 1.0 / x_ref.shape[-1]

    def rows(r, carry):
        sl = pl.ds(pl.multiple_of(r * SUBLANES, SUBLANES), SUBLANES)
        z = alpha * x_ref[sl, :] + gw_ref[...] * y_ref[sl, :]
        mu = jnp.sum(z, axis=-1, keepdims=True) * inv_d
        zc = z - mu
        var = jnp.sum(zc * zc, axis=-1, keepdims=True) * inv_d
        t = zc * lax.rsqrt(var + LN_EPS)
        xo_ref[sl, :] = t * g_ref[...] + b_ref[...]
        if with_u:
            uo_ref[sl, :] = (t * g2_ref[...] + b2_ref[...]).astype(uo_ref.dtype)
        return carry

    lax.fori_loop(0, tm // SUBLANES, rows, 0, unroll=2)


def ln_residual(x2, y2, mod, j_gate, ln_g, ln_b, j_ln, res_w, alpha, next_mod, next_j, *, rows_per_batch, tm):
    m, d = x2.shape
    bpb = rows_per_batch // tm
    with_u = next_mod is not None

    def mspec(j):
        return pl.BlockSpec((None, None, 1, d), lambda i: (i // bpb, j, 0, 0))

    row_spec = pl.BlockSpec((tm, d), lambda i: (i, 0))
    ln_spec = pl.BlockSpec((None, 1, d), lambda i: (j_ln, 0, 0))
    in_specs = [row_spec, row_spec, mspec(j_gate), ln_spec, ln_spec]
    args = [x2, y2, mod, ln_g.reshape(N_SUBLAYERS, 1, d), ln_b.reshape(N_SUBLAYERS, 1, d)]
    out_specs = [row_spec]
    out_shape = [jax.ShapeDtypeStruct((m, d), F32)]
    if with_u:
        j_shift, j_scale = next_j
        in_specs += [mspec(j_scale), mspec(j_shift)]
        args += [next_mod, next_mod]
        out_specs.append(row_spec)
        out_shape.append(jax.ShapeDtypeStruct((m, d), BF16))
    outs = pl.pallas_call(
        functools.partial(_ln_res_kernel, alpha=alpha, res_w=res_w, tm=tm, with_u=with_u),
        grid=(m // tm,),
        in_specs=in_specs,
        out_specs=out_specs,
        out_shape=out_shape,
        scratch_shapes=[pltpu.VMEM((1, d), F32)] * (3 if with_u else 1),
        compiler_params=_cparams(("parallel",)),
        name="ln_residual",
    )(*args)
    return (outs[0], outs[1]) if with_u else (outs[0], None)


def _dilated_kernel(slopes_ref, q_ref, kc_ref, kp_ref, vc_ref, vp_ref, o_ref,
                    kk_ref, vv_ref, bias_ref, *acc_refs, span, steps, patterns, scale):
    h = pl.program_id(1)
    i = pl.program_id(2)
    slope = slopes_ref[h]
    n_pat = len(patterns)
    o_acc = acc_refs[:n_pat]
    m_acc = acc_refs[n_pat:2 * n_pat]
    l_acc = acc_refs[2 * n_pat:]
    dh = q_ref.shape[-1]

    kk_ref[0:span, :] = kp_ref[...]
    kk_ref[span:2 * span, :] = kc_ref[...]
    vv_ref[0:span, :] = vp_ref[...]
    vv_ref[span:2 * span, :] = vc_ref[...]

    qi = lax.broadcasted_iota(jnp.int32, (steps, 2 * steps), 0)
    kj = lax.broadcasted_iota(jnp.int32, (steps, 2 * steps), 1)
    back = steps + qi - kj
    in_band = (back >= 0) & (back <= steps)
    backf = back.astype(F32)
    for g, (_, dil) in enumerate(patterns):
        bias = jnp.where(in_band, (-(slope * dil)) * backf, NEG_INF)
        bias_ref[2 * g] = bias
        bias_ref[2 * g + 1] = jnp.where(kj >= steps, bias, NEG_INF)
    ones_col = jnp.where(lax.broadcasted_iota(jnp.int32, (2 * steps, dh), 1) == 0, 1.0, 0.0).astype(BF16)

    for g, (_, dil) in enumerate(patterns):
        def tile(t, carry, g=g, dil=dil):
            r = t % dil
            n = t // dil
            q_start = r + dil * steps * n
            q_rows = pl.ds(q_start, steps, stride=dil)
            k_rows = pl.ds(span + q_start - dil * steps, 2 * steps, stride=dil)
            q = (q_ref[q_rows, :] * scale).astype(BF16)
            k = kk_ref[k_rows, :].astype(BF16)
            v1 = jnp.concatenate([vv_ref[k_rows, :].astype(BF16), ones_col], axis=1)
            no_prev = jnp.where((n + i) > 0, 0, 1)
            s = lax.dot_general(q, k, NT_DIMS, preferred_element_type=F32) + bias_ref[2 * g + no_prev]
            mx = jnp.max(s, axis=1, keepdims=True)
            p = jnp.exp(s - mx)
            ov = jnp.dot(p.astype(BF16), v1, preferred_element_type=F32)
            o_acc[g][q_rows, :] = ov[:, :dh]
            m_acc[g][q_rows, :] = mx
            l_acc[g][q_rows, :] = ov[:, dh:dh + 1]
            return carry

        lax.fori_loop(0, span // steps, tile, 0, unroll=4)

    def merge(c, carry):
        sl = pl.ds(pl.multiple_of(c * steps, steps), steps)
        ms = [m[sl, :] for m in m_acc]
        top = functools.reduce(jnp.maximum, ms)
        num = jnp.zeros((steps, o_ref.shape[-1]), F32)
        den = jnp.zeros((steps, 1), F32)
        for g in range(n_pat):
            w = jnp.exp(ms[g] - top)
            num = num + w * o_acc[g][sl, :]
            den = den + w * l_acc[g][sl, :]
        o_ref[sl, :] = (num / den).astype(o_ref.dtype)
        return carry

    lax.fori_loop(0, span // steps, merge, 0, unroll=2)


def dilated_attention(qkv, n_heads, *, q_col, k_col, v_col):
    b, s, _ = qkv.shape
    dh = HEAD_DIM
    span, steps = A_SPAN, A_STEPS
    n_pat = len(A_PATTERNS)

    def cur(col):
        return pl.BlockSpec((None, span, dh), lambda bb, h, i: (bb, i, col + h))

    def prev(col):
        return pl.BlockSpec((None, span, dh), lambda bb, h, i: (bb, jnp.maximum(i - 1, 0), col + h))

    scratch = [pltpu.VMEM((2 * span, dh), F32), pltpu.VMEM((2 * span, dh), F32),
               pltpu.VMEM((2 * n_pat, steps, 2 * steps), F32)]
    scratch += [pltpu.VMEM((span, dh), F32) for _ in range(n_pat)]
    scratch += [pltpu.VMEM((span, 1), F32) for _ in range(2 * n_pat)]
    return pl.pallas_call(
        functools.partial(_dilated_kernel, span=span, steps=steps, patterns=A_PATTERNS, scale=dh ** -0.5),
        grid=(b, n_heads, s // span),
        in_specs=[pl.BlockSpec(memory_space=pltpu.SMEM),
                  cur(q_col), cur(k_col), prev(k_col), cur(v_col), prev(v_col)],
        out_specs=pl.BlockSpec((None, span, dh), lambda bb, h, i: (bb, i, h)),
        out_shape=jax.ShapeDtypeStruct((b, s, n_heads * dh), BF16),
        scratch_shapes=scratch,
        compiler_params=_cparams(("parallel", "parallel", "arbitrary")),
        name="dilated_attention",
    )(_alibi_slopes(n_heads), qkv, qkv, qkv, qkv, qkv)


def _conv4_kernel(x_ref, halo_ref, w_ref, o_ref, buf_ref, *, t_rows, width, k_blocks_from, k_scale):
    i = pl.program_id(1)
    j = pl.program_id(2)
    halo = jnp.where(i > 0, halo_ref[...], 0.0)
    buf_ref[0:SUBLANES, :] = halo
    buf_ref[SUBLANES:SUBLANES + t_rows, :] = x_ref[...]
    off = SUBLANES - (width - 1)
    acc = jnp.zeros(x_ref.shape, F32)
    for tap in range(width):
        acc = acc + w_ref[tap:tap + 1, :] * buf_ref[off + tap:off + tap + t_rows, :]
    y = acc * _sigmoid(acc)
    y = y * jnp.where(j >= k_blocks_from, k_scale, 1.0)
    o_ref[...] = y.astype(o_ref.dtype)


def conv4_silu(qkv, conv_w, *, col0, width_cols, k_scale, t_rows, tc):
    b, s, _ = qkv.shape
    width = conv_w.shape[0]
    cb0 = col0 // tc
    n_cb = width_cols // tc
    rb = t_rows // SUBLANES
    return pl.pallas_call(
        functools.partial(_conv4_kernel, t_rows=t_rows, width=width, k_blocks_from=n_cb // 2, k_scale=k_scale),
        grid=(b, s // t_rows, n_cb),
        in_specs=[pl.BlockSpec((None, t_rows, tc), lambda bb, i, j: (bb, i, cb0 + j)),
                  pl.BlockSpec((None, SUBLANES, tc), lambda bb, i, j: (bb, jnp.maximum(i * rb - 1, 0), cb0 + j)),
                  pl.BlockSpec((width, tc), lambda bb, i, j: (0, j))],
        out_specs=pl.BlockSpec((None, t_rows, tc), lambda bb, i, j: (bb, i, j)),
        out_shape=jax.ShapeDtypeStruct((b, s, width_cols), BF16),
        scratch_shapes=[pltpu.VMEM((SUBLANES + t_rows, tc), F32)],
        compiler_params=_cparams(("parallel", "parallel", "parallel")),
        name="conv4_silu",
    )(qkv, qkv, conv_w)


def _mlstm_kernel(big_ref, bfg_ref, gates_ref, q_ref, k_ref, v_ref, og_ref, ng_ref, o_ref,
                  c_ref, n_ref, m_ref, *, chunk):
    h = pl.program_id(1)
    c = pl.program_id(2)

    @pl.when(c == 0)
    def _init():
        c_ref[...] = jnp.zeros_like(c_ref)
        n_ref[...] = jnp.zeros_like(n_ref)
        m_ref[...] = jnp.zeros_like(m_ref)

    gates = gates_ref[...]
    i_row = gates[0:1, :] + big_ref[h]
    f_row = gates[1:2, :] + bfg_ref[h]
    lf_row = jnp.minimum(f_row, 0.0) - jnp.log(1.0 + jnp.exp(-jnp.abs(f_row)))

    tt = lax.broadcasted_iota(jnp.int32, (chunk, chunk), 0)
    ss = lax.broadcasted_iota(jnp.int32, (chunk, chunk), 1)
    causal = ss <= tt
    diag = ss == tt
    lf_b = jnp.broadcast_to(lf_row, (chunk, chunk))
    cum_col = jnp.sum(jnp.where(causal, lf_b, 0.0), axis=1, keepdims=True)
    lf_col = jnp.sum(jnp.where(diag, lf_b, 0.0), axis=1, keepdims=True)
    i_col = jnp.sum(jnp.where(diag, jnp.broadcast_to(i_row, (chunk, chunk)), 0.0), axis=1, keepdims=True)
    cum_row = jnp.sum(jnp.where(tt <= ss, jnp.broadcast_to(lf_col, (chunk, chunk)), 0.0),
                      axis=0, keepdims=True)
    cum_last = jnp.sum(lf_row, axis=1, keepdims=True)
    m_prev = m_ref[...]

    log_w = jnp.where(causal, cum_col - cum_row + i_row, NEG_INF)
    log_inter = cum_col + m_prev
    m_row = jnp.maximum(log_inter, jnp.max(log_w, axis=1, keepdims=True))
    w_intra = jnp.exp(log_w - m_row)
    w_inter = jnp.exp(log_inter - m_row)

    q = q_ref[...]
    k = k_ref[...]
    v = v_ref[...].astype(BF16)
    c_state = c_ref[...]
    n_state = n_ref[...]
    attn = w_intra * lax.dot_general(q, k, NT_DIMS, preferred_element_type=F32)
    num = (w_inter * jnp.dot(q, c_state.astype(BF16), preferred_element_type=F32)
           + jnp.dot(attn.astype(BF16), v, preferred_element_type=F32))
    den = (w_inter * jnp.sum(q.astype(F32) * n_state, axis=1, keepdims=True)
           + jnp.sum(attn, axis=1, keepdims=True))
    h_out = num / jnp.maximum(jnp.abs(den), jnp.exp(-m_row))

    lte_col = cum_last - cum_col + i_col
    lte_row = cum_last - cum_row + i_row
    m_new = jnp.maximum(cum_last + m_prev, jnp.max(lte_row, axis=1, keepdims=True))
    w_end = jnp.exp(lte_col - m_new)
    decay = jnp.exp(cum_last + m_prev - m_new)
    kw = k.astype(F32) * w_end
    c_ref[...] = decay * c_state + jnp.dot(kw.T.astype(BF16), v, preferred_element_type=F32)
    n_ref[...] = decay * n_state + jnp.sum(kw, axis=0, keepdims=True)
    m_ref[...] = m_new

    mu = jnp.mean(h_out, axis=1, keepdims=True)
    hc = h_out - mu
    var = jnp.mean(hc * hc, axis=1, keepdims=True)
    h_norm = hc * lax.rsqrt(var + LN_EPS) * ng_ref[...]
    o_ref[...] = (_sigmoid(og_ref[...]) * h_norm).astype(o_ref.dtype)


def mlstm(qkv, qk_conv, gates, b_igate, b_fgate, norm_g, *, v_col0, o_col0):
    b, s, _ = qkv.shape
    n_heads = gates.shape[1]
    dk = qk_conv.shape[-1] // (2 * n_heads)
    dv = norm_g.shape[-1] // n_heads
    chunk = B_CHUNK
    return pl.pallas_call(
        functools.partial(_mlstm_kernel, chunk=chunk),
        grid=(b, n_heads, s // chunk),
        in_specs=[pl.BlockSpec(memory_space=pltpu.SMEM),
                  pl.BlockSpec(memory_space=pltpu.SMEM),
                  pl.BlockSpec((None, None, 2, chunk), lambda bb, h, c: (bb, h, 0, c)),
                  pl.BlockSpec((None, chunk, dk), lambda bb, h, c: (bb, c, h)),
                  pl.BlockSpec((None, chunk, dk), lambda bb, h, c: (bb, c, n_heads + h)),
                  pl.BlockSpec((None, chunk, dv), lambda bb, h, c: (bb, c, v_col0 // dv + h)),
                  pl.BlockSpec((None, chunk, dv), lambda bb, h, c: (bb, c, o_col0 // dv + h)),
                  pl.BlockSpec((1, dv), lambda bb, h, c: (0, h))],
        out_specs=pl.BlockSpec((None, chunk, dv), lambda bb, h, c: (bb, c, h)),
        out_shape=jax.ShapeDtypeStruct((b, s, n_heads * dv), BF16),
        scratch_shapes=[pltpu.VMEM((dk, dv), F32), pltpu.VMEM((1, dk), F32), pltpu.VMEM((1, 1), F32)],
        compiler_params=_cparams(("parallel", "parallel", "arbitrary")),
        name="mlstm",
    )(b_igate, b_fgate, gates, qk_conv, qk_conv, qkv, qkv, norm_g.reshape(1, -1))


MOBA_PENALTY = -3.0e38
MOBA_BIAS_PARTS = 3
MOBA_GROUP = 4
MOBA_KEY_CHUNK = 1024


def _moba_kernel(slopes_ref, q_ref, k_ref, v_ref, o_ref, kaug_ref, vaug_ref, kmean_ref, pen_ref,
                 *, blk, n_blk, top_k, scale):
    h = pl.program_id(1)
    slope = slopes_ref[h]
    dh = q_ref.shape[-1]
    lane = lax.broadcasted_iota(jnp.int32, (blk, LANES), 1)
    row = lax.broadcasted_iota(jnp.int32, (blk, 1), 0)

    kmean_ref[...] = jnp.zeros_like(kmean_ref)
    ones_col = jnp.where(lane == 0, 1.0, 0.0).astype(BF16)
    for n in range(n_blk):
        rows = slice(n * blk, (n + 1) * blk)
        kb = k_ref[rows, :]
        kmean_ref[n:n + 1, :] = jnp.mean(kb, axis=0, keepdims=True)
        aug = jnp.where(lane == n, 1.0, 0.0)
        rem = slope * (row + n * blk).astype(F32)
        for part in range(MOBA_BIAS_PARTS):
            piece = rem.astype(BF16).astype(F32)
            aug = aug + jnp.where(lane == n_blk + part, piece, 0.0)
            rem = rem - piece
        kaug_ref[rows, 0:dh] = kb.astype(BF16)
        kaug_ref[rows, dh:2 * dh] = aug.astype(BF16)
        vaug_ref[rows, 0:dh] = v_ref[rows, :].astype(BF16)
        vaug_ref[rows, dh:2 * dh] = ones_col

    ri = lax.broadcasted_iota(jnp.int32, (blk, blk), 0)
    cj = lax.broadcasted_iota(jnp.int32, (blk, blk), 1)

    blk_id = lax.broadcasted_iota(jnp.int32, (n_blk, blk), 0)
    tail_id = lax.broadcasted_iota(jnp.int32, (LANES - n_blk, blk), 0)
    tail = jnp.where(tail_id < MOBA_BIAS_PARTS, 1.0, 0.0)

    def select(qb, carry):
        q_rows = pl.ds(pl.multiple_of(qb * blk, blk), blk)
        gate = lax.dot_general(kmean_ref[0:n_blk, :], q_ref[q_rows, :] * scale, NT_DIMS,
                               precision=lax.Precision.HIGHEST, preferred_element_type=F32)
        gate = jnp.where(blk_id < qb, gate, NEG_INF)
        sel = jnp.zeros(gate.shape, F32)
        for j in range(top_k):
            mx = jnp.max(gate, axis=0, keepdims=True)
            idx = jnp.min(jnp.where(gate == mx, blk_id, n_blk), axis=0, keepdims=True)
            pick = blk_id == idx
            sel = sel + jnp.where(pick, jnp.where(qb > j, 1.0, 0.0), 0.0)
            gate = jnp.where(pick, NEG_INF, gate)
        pen_t = jnp.concatenate([jnp.where(sel > 0.0, 0.0, MOBA_PENALTY), tail], axis=0)
        pen_ref[qb] = pen_t.T.astype(BF16)
        return carry

    lax.fori_loop(0, n_blk, select, 0, unroll=MOBA_GROUP)

    def make_attend(n_keys):
        def attend(qb, carry):
            q_rows = pl.ds(pl.multiple_of(qb * blk, blk), blk)
            q16 = (q_ref[q_rows, :] * scale).astype(BF16)
            pen = pen_ref[qb]
            q_past = jnp.concatenate([q16, pen], axis=1)
            q_own = jnp.concatenate([q16, jnp.where(lane == qb, 0.0, pen.astype(F32)).astype(BF16)], axis=1)
            s_own = lax.dot_general(q_own, kaug_ref[q_rows, :], NT_DIMS, preferred_element_type=F32)
            s_own = jnp.where(ri >= cj, s_own, NEG_INF)
            m = jnp.max(s_own, axis=1, keepdims=True)
            acc = jnp.dot(jnp.exp(s_own - m).astype(BF16), vaug_ref[q_rows, :], preferred_element_type=F32)
            for k0 in range(0, n_keys, MOBA_KEY_CHUNK):
                keys = slice(k0, min(k0 + MOBA_KEY_CHUNK, n_keys))
                s = lax.dot_general(q_past, kaug_ref[keys, :], NT_DIMS, preferred_element_type=F32)
                m_new = jnp.maximum(m, jnp.max(s, axis=1, keepdims=True))
                acc = (jnp.exp(m - m_new) * acc
                       + jnp.dot(jnp.exp(s - m_new).astype(BF16), vaug_ref[keys, :], preferred_element_type=F32))
                m = m_new
            o_ref[q_rows, :] = (acc[:, :dh] / acc[:, dh:dh + 1]).astype(o_ref.dtype)
            return carry
        return attend

    for first in range(0, n_blk, MOBA_GROUP):
        last = min(first + MOBA_GROUP, n_blk)
        lax.fori_loop(first, last, make_attend(last * blk), 0, unroll=2)


def moba_attention(qkv, n_heads, *, q_col, k_col, v_col):
    b, s, _ = qkv.shape
    dh = HEAD_DIM
    blk = MOBA_BLOCK
    n_blk = s // blk
    top_k = min(MOBA_TOPK, n_blk)
    assert n_blk + MOBA_BIAS_PARTS <= LANES and n_blk % SUBLANES == 0 and dh == LANES

    def head(col):
        return pl.BlockSpec((None, s, dh), lambda bb, h: (bb, 0, col + h))

    return pl.pallas_call(
        functools.partial(_moba_kernel, blk=blk, n_blk=n_blk, top_k=top_k, scale=dh ** -0.5),
        grid=(b, n_heads),
        in_specs=[pl.BlockSpec(memory_space=pltpu.SMEM), head(q_col), head(k_col), head(v_col)],
        out_specs=pl.BlockSpec((None, s, dh), lambda bb, h: (bb, 0, h)),
        out_shape=jax.ShapeDtypeStruct((b, s, n_heads * dh), BF16),
        scratch_shapes=[pltpu.VMEM((s, 2 * dh), BF16), pltpu.VMEM((s, 2 * dh), BF16),
                        pltpu.VMEM((LANES, dh), F32), pltpu.VMEM((n_blk, blk, LANES), BF16)],
        compiler_params=_cparams(("parallel", "parallel")),
        name="moba_attention",
    )(_alibi_slopes(n_heads), qkv, qkv, qkv)


CONV_HALO = 32
LN_ROWS = 16


def _conv31_kernel(a_ref, ah_ref, b_ref, bh_ref, w_ref, g_ref, beta_ref, o_ref, glu_ref, conv_ref, shift_ref,
                   *, t_rows, width, channels):
    i = pl.program_id(1)
    halo = ah_ref[...] * _sigmoid(bh_ref[...])
    glu_ref[0:CONV_HALO, :] = jnp.where(i > 0, halo, 0.0)
    glu_ref[CONV_HALO:CONV_HALO + t_rows, :] = a_ref[...] * _sigmoid(b_ref[...])
    off = CONV_HALO - (width - 1)

    strip_rows = CONV_HALO + t_rows - SUBLANES

    def cols(c, carry):
        cs = pl.ds(pl.multiple_of(c * LANES, LANES), LANES)
        for phase in range(1, SUBLANES):
            shift_ref[phase, 0:strip_rows, :] = glu_ref[phase:phase + strip_rows, cs]
        acc = jnp.zeros((t_rows, LANES), F32)
        for tap in range(width):
            phase, base = (off + tap) % SUBLANES, (off + tap) // SUBLANES * SUBLANES
            src = glu_ref[base:base + t_rows, cs] if phase == 0 else shift_ref[phase, base:base + t_rows, :]
            acc = acc + w_ref[tap:tap + 1, cs] * src
        conv_ref[:, cs] = acc
        return carry

    lax.fori_loop(0, channels // LANES, cols, 0)

    def rows(r, carry):
        rs = pl.ds(pl.multiple_of(r * LN_ROWS, LN_ROWS), LN_ROWS)
        y = _ln_rows(conv_ref[rs, :], g_ref[...], beta_ref[...])
        o_ref[rs, :] = (y * _sigmoid(y)).astype(o_ref.dtype)
        return carry

    lax.fori_loop(0, t_rows // LN_ROWS, rows, 0, unroll=2)


def glu_conv_ln_silu(qkv, conv_w, ln_g, ln_b, *, a_col0, b_col0, channels, t_rows):
    b, s, _ = qkv.shape
    width = conv_w.shape[0]
    hb = t_rows // CONV_HALO
    ca, cb = a_col0 // channels, b_col0 // channels

    def cur(col):
        return pl.BlockSpec((None, t_rows, channels), lambda bb, i: (bb, i, col))

    def halo(col):
        return pl.BlockSpec((None, CONV_HALO, channels), lambda bb, i: (bb, jnp.maximum(i * hb - 1, 0), col))

    vec = pl.BlockSpec((1, channels), lambda bb, i: (0, 0))
    return pl.pallas_call(
        functools.partial(_conv31_kernel, t_rows=t_rows, width=width, channels=channels),
        grid=(b, s // t_rows),
        in_specs=[cur(ca), halo(ca), cur(cb), halo(cb),
                  pl.BlockSpec((width, channels), lambda bb, i: (0, 0)), vec, vec],
        out_specs=pl.BlockSpec((None, t_rows, channels), lambda bb, i: (bb, i, 0)),
        out_shape=jax.ShapeDtypeStruct((b, s, channels), BF16),
        scratch_shapes=[pltpu.VMEM((CONV_HALO + t_rows, channels), F32), pltpu.VMEM((t_rows, channels), F32),
                        pltpu.VMEM((SUBLANES, CONV_HALO + t_rows - SUBLANES, LANES), F32)],
        compiler_params=_cparams(("parallel", "parallel")),
        name="glu_conv_ln_silu",
    )(qkv, qkv, qkv, qkv, conv_w, ln_g.reshape(1, -1), ln_b.reshape(1, -1))


DOWN_TILES = dict(tm=2048, tn=1024, tk=1024)
PANEL_TILES = dict(tm=2048, tn=256)
ROW_TILE = 256


def _ffn(u, w_up, w_down):
    hmid = matmul_swiglu(u, w_up, **PANEL_TILES)
    return matmul_f32(hmid, w_down, w_down.shape[1], **DOWN_TILES)


def _mixer_dilated_mlstm(u, bsz, seq, w_in, conv_qk, b_igate, b_fgate, norm_g, w_out):
    d = u.shape[1]
    a_width = d // 2
    a_heads = a_width // HEAD_DIM
    b_width = d - a_width
    qk_width = b_width // 2
    n_main = 3 * a_width + 2 * qk_width + 2 * b_width
    proj = matmul_rows([u], w_in, n_main, single_buffer_a=True, **PANEL_TILES).reshape(bsz, seq, n_main)
    gates = matmul_rows([u], w_in, LANES, tm=PANEL_TILES["tm"], tn=LANES, col_block0=n_main // LANES,
                        single_buffer_a=True)
    gates = gates[:, :2 * B_HEADS].reshape(bsz, seq, 2, B_HEADS).transpose(0, 3, 2, 1)

    hb = a_width // HEAD_DIM
    a_out = dilated_attention(proj, a_heads, q_col=0, k_col=hb, v_col=2 * hb)
    qk_col0 = 3 * a_width
    qk_conv = conv4_silu(proj, conv_qk, col0=qk_col0, width_cols=2 * qk_width,
                         k_scale=(qk_width // B_HEADS) ** -0.5, t_rows=512, tc=512)
    b_out = mlstm(proj, qk_conv, gates, b_igate, b_fgate, norm_g,
                  v_col0=qk_col0 + 2 * qk_width, o_col0=qk_col0 + 2 * qk_width + b_width)
    halves = [t.reshape(bsz * seq, -1) for t in (a_out, b_out)]
    return matmul_rows(halves, w_out, d, single_buffer_a=True, **PANEL_TILES)


def _mixer_moba_conv(u, bsz, seq, w_in, conv_dw, conv_ln_g, conv_ln_b, w_out):
    d = u.shape[1]
    c_width = d // 2
    c_heads = c_width // HEAD_DIM
    d_ch = d - c_width
    n_main = 3 * c_width + 2 * d_ch
    proj = matmul_rows([u], w_in, n_main, single_buffer_a=True, **PANEL_TILES).reshape(bsz, seq, n_main)
    c_out = moba_attention(proj, c_heads, q_col=0, k_col=c_heads, v_col=2 * c_heads)
    d_out = glu_conv_ln_silu(proj, conv_dw, conv_ln_g, conv_ln_b, a_col0=3 * c_width,
                             b_col0=3 * c_width + d_ch, channels=d_ch, t_rows=ROW_TILE)
    halves = [t.reshape(bsz * seq, -1) for t in (c_out, d_out)]
    return matmul_rows(halves, w_out, d, single_buffer_a=True, **PANEL_TILES)


def kernel(x, c, l0_w_ada, l0_b_ada, l0_ln_g, l0_ln_b, l0_ffn1_w_up, l0_ffn1_w_down, l0_ffn2_w_up, l0_ffn2_w_down, l0_w_in, l0_conv_qk, l0_b_igate, l0_b_fgate, l0_mlstm_norm_g, l0_w_out, l1_w_ada, l1_b_ada, l1_ln_g, l1_ln_b, l1_ffn1_w_up, l1_ffn1_w_down, l1_ffn2_w_up, l1_ffn2_w_down, l1_w_in, l1_conv_dw, l1_conv_ln_g, l1_conv_ln_b, l1_w_out):
    bsz, seq, d = x.shape
    depth = 2
    alpha = (2 * depth) ** 0.25
    layers = (
        (l0_w_ada, l0_b_ada, l0_ln_g, l0_ln_b, l0_ffn1_w_up, l0_ffn1_w_down, l0_ffn2_w_up, l0_ffn2_w_down),
        (l1_w_ada, l1_b_ada, l1_ln_g, l1_ln_b, l1_ffn1_w_up, l1_ffn1_w_down, l1_ffn2_w_up, l1_ffn2_w_down),
    )
    mixers = (
        lambda u: _mixer_dilated_mlstm(u, bsz, seq, l0_w_in, l0_conv_qk, l0_b_igate, l0_b_fgate,
                                       l0_mlstm_norm_g, l0_w_out),
        lambda u: _mixer_moba_conv(u, bsz, seq, l1_w_in, l1_conv_dw, l1_conv_ln_g, l1_conv_ln_b, l1_w_out),
    )
    c_pad = jnp.pad(c, ((0, SUBLANES - bsz), (0, 0)))
    mods = [ada_modulation(c_pad, p[0], p[1], tn=512)[:bsz].reshape(bsz, 3 * N_SUBLAYERS, 1, d) for p in layers]

    x2 = x.reshape(bsz * seq, d)
    u = modulate(x2, mods[0], 0, 1, rows_per_batch=seq, tm=ROW_TILE)
    for layer in range(depth):
        _, _, ln_g, ln_b, f1_up, f1_down, f2_up, f2_down = layers[layer]
        branches = (
            (0.5, lambda v: _ffn(v, f1_up, f1_down)),
            (1.0, mixers[layer]),
            (0.5, lambda v: _ffn(v, f2_up, f2_down)),
        )
        for j, (res_w, fn) in enumerate(branches):
            y = fn(u)
            if j + 1 < N_SUBLAYERS:
                nxt, nj = mods[layer], (3 * (j + 1), 3 * (j + 1) + 1)
            elif layer + 1 < depth:
                nxt, nj = mods[layer + 1], (0, 1)
            else:
                nxt, nj = None, None
            x2, u = ln_residual(x2, y, mods[layer], 3 * j + 2, ln_g, ln_b, j, res_w, alpha, nxt, nj,
                                rows_per_batch=seq, tm=ROW_TILE)
    return x2.reshape(bsz, seq, d)
```

```python
import functools

import jax
import jax.numpy as jnp
import numpy as np
from jax import lax
from jax.experimental import pallas as pl
from jax.experimental.pallas import tpu as pltpu

F32 = jnp.float32
BF16 = jnp.bfloat16
NEG_INF = float("-inf")

LANES = 128
SUBLANES = 8
VMEM_LIMIT_BYTES = 56 * 1024 * 1024
SWIGLU_VMEM_LIMIT_BYTES = 60 * 1024 * 1024

HEAD_DIM = 128
A_PATTERNS = ((128, 1), (512, 4), (2048, 16))
A_STEPS = 128
A_SPAN = 2048
B_HEADS = 4
B_CONV_WIDTH = 4
B_CHUNK = 256
MOBA_BLOCK = 256
MOBA_TOPK = 3
D_CONV_WIDTH = 31
N_SUBLAYERS = 3
LN_EPS = 1e-5

NT_DIMS = (((1,), (1,)), ((), ()))


def _cparams(semantics, vmem_limit_bytes=VMEM_LIMIT_BYTES):
    return pltpu.CompilerParams(dimension_semantics=semantics, vmem_limit_bytes=vmem_limit_bytes)


def _sigmoid(x):
    return 1.0 / (1.0 + jnp.exp(-x))


def _alibi_slopes(n_heads):
    return jnp.asarray(2.0 ** (-8.0 * np.arange(1, n_heads + 1) / n_heads), F32)


def _mm_f32_kernel(a_ref, w_ref, o_ref):
    @pl.when(pl.program_id(2) == 0)
    def _init():
        o_ref[...] = jnp.zeros_like(o_ref)

    o_ref[...] += jnp.dot(a_ref[...], w_ref[...].astype(BF16), preferred_element_type=F32)


def matmul_f32(a, w, n_out, *, tm, tn, tk):
    m, k = a.shape
    grid = (m // tm, n_out // tn, k // tk)
    return pl.pallas_call(
        _mm_f32_kernel,
        grid=grid,
        in_specs=[pl.BlockSpec((tm, tk), lambda i, j, kk: (i, kk)),
                  pl.BlockSpec((tk, tn), lambda i, j, kk: (kk, j))],
        out_specs=pl.BlockSpec((tm, tn), lambda i, j, kk: (i, j)),
        out_shape=jax.ShapeDtypeStruct((m, n_out), F32),
        compiler_params=_cparams(("parallel", "parallel", "arbitrary")),
        name="matmul_f32",
    )(a, w)


def _mm_rows_kernel(*refs, n_a, w_is_transposed):
    a_refs, w_ref, o_ref = refs[:n_a], refs[n_a], refs[n_a + 1]
    acc, off = None, 0
    for a_ref in a_refs:
        ka = a_ref.shape[1]
        if w_is_transposed:
            part = lax.dot_general(a_ref[...], w_ref[:, off:off + ka].astype(BF16), NT_DIMS,
                                   preferred_element_type=F32)
        else:
            part = jnp.dot(a_ref[...], w_ref[off:off + ka, :].astype(BF16), preferred_element_type=F32)
        acc = part if acc is None else acc + part
        off += ka
    o_ref[...] = acc.astype(o_ref.dtype)


def matmul_rows(a_list, w, n_out, *, tm, tn, col_block0=0, single_buffer_a=False, w_is_transposed=False):
    m = a_list[0].shape[0]
    k = w.shape[1] if w_is_transposed else w.shape[0]
    assert sum(a.shape[1] for a in a_list) == k
    mode = dict(pipeline_mode=pl.Buffered(1)) if single_buffer_a else {}
    in_specs = [pl.BlockSpec((tm, a.shape[1]), lambda i, j: (i, 0), **mode) for a in a_list]
    if w_is_transposed:
        in_specs.append(pl.BlockSpec((tn, k), lambda i, j: (j + col_block0, 0)))
    else:
        in_specs.append(pl.BlockSpec((k, tn), lambda i, j: (0, j + col_block0)))
    return pl.pallas_call(
        functools.partial(_mm_rows_kernel, n_a=len(a_list), w_is_transposed=w_is_transposed),
        grid=(m // tm, pl.cdiv(n_out, tn)),
        in_specs=in_specs,
        out_specs=pl.BlockSpec((tm, tn), lambda i, j: (i, j)),
        out_shape=jax.ShapeDtypeStruct((m, n_out), F32),
        compiler_params=_cparams(("parallel", "arbitrary")),
        name="matmul_rows",
    )(*a_list, w)


def _mm_swiglu_kernel(a_ref, wg_ref, wv_ref, o_ref):
    a = a_ref[...]
    g = jnp.dot(a, wg_ref[...].astype(BF16), preferred_element_type=F32)
    v = jnp.dot(a, wv_ref[...].astype(BF16), preferred_element_type=F32)
    o_ref[...] = (g * _sigmoid(g) * v).astype(o_ref.dtype)


def matmul_swiglu(a, w_up, *, tm, tn):
    m, k = a.shape
    f = w_up.shape[1] // 2
    nh = f // tn
    return pl.pallas_call(
        _mm_swiglu_kernel,
        grid=(m // tm, nh),
        in_specs=[pl.BlockSpec((tm, k), lambda i, j: (i, 0)),
                  pl.BlockSpec((k, tn), lambda i, j: (0, j)),
                  pl.BlockSpec((k, tn), lambda i, j: (0, j + nh))],
        out_specs=pl.BlockSpec((tm, tn), lambda i, j: (i, j)),
        out_shape=jax.ShapeDtypeStruct((m, f), BF16),
        compiler_params=_cparams(("parallel", "arbitrary"), SWIGLU_VMEM_LIMIT_BYTES),
        name="matmul_swiglu",
    )(a, w_up, w_up)


def _ada_kernel(c_ref, w_ref, b_ref, o_ref):
    c = c_ref[...]
    a = (c * _sigmoid(c)).astype(BF16)
    o_ref[...] = jnp.dot(a, w_ref[...].astype(BF16), preferred_element_type=F32) + b_ref[...]


def ada_modulation(c_pad, w_ada, b_ada, *, tn):
    rows, d = c_pad.shape
    n = w_ada.shape[1]
    return pl.pallas_call(
        _ada_kernel,
        grid=(n // tn,),
        in_specs=[pl.BlockSpec((rows, d), lambda j: (0, 0)),
                  pl.BlockSpec((d, tn), lambda j: (0, j)),
                  pl.BlockSpec((1, tn), lambda j: (0, j))],
        out_specs=pl.BlockSpec((rows, tn), lambda j: (0, j)),
        out_shape=jax.ShapeDtypeStruct((rows, n), F32),
        compiler_params=_cparams(("parallel",)),
        name="ada_modulation",
    )(c_pad, w_ada, b_ada.reshape(1, n))


def _modulate_kernel(x_ref, sc_ref, sh_ref, u_ref):
    u_ref[...] = (x_ref[...] * (1.0 + sc_ref[...]) + sh_ref[...]).astype(u_ref.dtype)


def modulate(x2, mod, j_shift, j_scale, *, rows_per_batch, tm):
    m, d = x2.shape
    bpb = rows_per_batch // tm

    def mspec(j):
        return pl.BlockSpec((None, None, 1, d), lambda i: (i // bpb, j, 0, 0))

    return pl.pallas_call(
        _modulate_kernel,
        grid=(m // tm,),
        in_specs=[pl.BlockSpec((tm, d), lambda i: (i, 0)), mspec(j_scale), mspec(j_shift)],
        out_specs=pl.BlockSpec((tm, d), lambda i: (i, 0)),
        out_shape=jax.ShapeDtypeStruct((m, d), BF16),
        compiler_params=_cparams(("parallel",)),
        name="modulate",
    )(x2, mod, mod)


def _ln_rows(z, g, b):
    mu = jnp.mean(z, axis=-1, keepdims=True)
    zc = z - mu
    var = jnp.mean(zc * zc, axis=-1, keepdims=True)
    return zc * lax.rsqrt(var + LN_EPS) * g + b


def _ln_res_kernel(x_ref, y_ref, gate_ref, g_ref, b_ref, *rest, alpha, res_w, tm, with_u):
    tile = (SUBLANES, x_ref.shape[-1])
    if with_u:
        sc_ref, sh_ref, xo_ref, uo_ref, gw_ref, g1_ref, b1_ref, g2_ref, b2_ref = rest
        g2_ref[...] = jnp.broadcast_to(g_ref[...] * (1.0 + sc_ref[...]), tile)
        b2_ref[...] = jnp.broadcast_to(b_ref[...] * (1.0 + sc_ref[...]) + sh_ref[...], tile)
    else:
        xo_ref, gw_ref, g1_ref, b1_ref = rest
    gw_ref[...] = jnp.broadcast_to(res_w * (1.0 + gate_ref[...]), tile)
    g1_ref[...] = jnp.broadcast_to(g_ref[...], tile)
    b1_ref[...] = jnp.broadcast_to(b_ref[...], tile)
    inv_d = 1.0 / x_ref.shape[-1]

    def rows(r, carry):
        sl = pl.ds(pl.multiple_of(r * SUBLANES, SUBLANES), SUBLANES)
        z = alpha * x_ref[sl, :] + gw_ref[...] * y_ref[sl, :]
        mu = jnp.sum(z, axis=-1, keepdims=True) * inv_d
        zc = z - mu
        var = jnp.sum(zc * zc, axis=-1, keepdims=True) * inv_d
        t = zc * lax.rsqrt(var + LN_EPS)
        xo_ref[sl, :] = t * g1_ref[...] + b1_ref[...]
        if with_u:
            uo_ref[sl, :] = (t * g2_ref[...] + b2_ref[...]).astype(uo_ref.dtype)
        return carry

    lax.fori_loop(0, tm // SUBLANES, rows, 0, unroll=4)


def ln_residual(x2, y2, mod, j_gate, ln_g, ln_b, j_ln, res_w, alpha, next_mod, next_j, *, rows_per_batch, tm):
    m, d = x2.shape
    bpb = rows_per_batch // tm
    with_u = next_mod is not None

    def mspec(j):
        return pl.BlockSpec((None, None, 1, d), lambda i: (i // bpb, j, 0, 0))

    row_spec = pl.BlockSpec((tm, d), lambda i: (i, 0))
    ln_spec = pl.BlockSpec((None, 1, d), lambda i: (j_ln, 0, 0))
    in_specs = [row_spec, row_spec, mspec(j_gate), ln_spec, ln_spec]
    args = [x2, y2, mod, ln_g.reshape(N_SUBLAYERS, 1, d), ln_b.reshape(N_SUBLAYERS, 1, d)]
    out_specs = [row_spec]
    out_shape = [jax.ShapeDtypeStruct((m, d), F32)]
    if with_u:
        j_shift, j_scale = next_j
        in_specs += [mspec(j_scale), mspec(j_shift)]
        args += [next_mod, next_mod]
        out_specs.append(row_spec)
        out_shape.append(jax.ShapeDtypeStruct((m, d), BF16))
    outs = pl.pallas_call(
        functools.partial(_ln_res_kernel, alpha=alpha, res_w=res_w, tm=tm, with_u=with_u),
        grid=(m // tm,),
        in_specs=in_specs,
        out_specs=out_specs,
        out_shape=out_shape,
        scratch_shapes=[pltpu.VMEM((SUBLANES, d), F32)] * (5 if with_u else 3),
        compiler_params=_cparams(("parallel",)),
        name="ln_residual",
    )(*args)
    return (outs[0], outs[1]) if with_u else (outs[0], None)


def _dilated_kernel(slopes_ref, q_ref, kc_ref, kp_ref, vc_ref, vp_ref, o_ref,
                    kk_ref, vv_ref, bias_ref, *acc_refs, span, steps, patterns, scale):
    h = pl.program_id(1)
    i = pl.program_id(2)
    slope = slopes_ref[h]
    n_pat = len(patterns)
    o_acc = acc_refs[:n_pat]
    m_acc = acc_refs[n_pat:2 * n_pat]
    l_acc = acc_refs[2 * n_pat:]
    dh = q_ref.shape[-1]

    kk_ref[0:span, :] = kp_ref[...]
    kk_ref[span:2 * span, :] = kc_ref[...]
    vv_ref[0:span, :] = vp_ref[...]
    vv_ref[span:2 * span, :] = vc_ref[...]

    qi = lax.broadcasted_iota(jnp.int32, (steps, 2 * steps), 0)
    kj = lax.broadcasted_iota(jnp.int32, (steps, 2 * steps), 1)
    back = steps + qi - kj
    in_band = (back >= 0) & (back <= steps)
    backf = back.astype(F32)
    for g, (_, dil) in enumerate(patterns):
        bias = jnp.where(in_band, (-(slope * dil)) * backf, NEG_INF)
        bias_ref[2 * g] = bias
        bias_ref[2 * g + 1] = jnp.where(kj >= steps, bias, NEG_INF)
    ones_col = jnp.where(lax.broadcasted_iota(jnp.int32, (2 * steps, dh), 1) == 0, 1.0, 0.0).astype(BF16)

    for g, (_, dil) in enumerate(patterns):
        def tile(t, carry, g=g, dil=dil):
            r = t % dil
            n = t // dil
            q_start = r + dil * steps * n
            q_rows = pl.ds(q_start, steps, stride=dil)
            k_rows = pl.ds(span + q_start - dil * steps, 2 * steps, stride=dil)
            q = (q_ref[q_rows, :] * scale).astype(BF16)
            k = kk_ref[k_rows, :].astype(BF16)
            v1 = jnp.concatenate([vv_ref[k_rows, :].astype(BF16), ones_col], axis=1)
            no_prev = jnp.where((n + i) > 0, 0, 1)
            s = lax.dot_general(q, k, NT_DIMS, preferred_element_type=F32) + bias_ref[2 * g + no_prev]
            mx = jnp.max(s, axis=1, keepdims=True)
            p = jnp.exp(s - mx)
            ov = jnp.dot(p.astype(BF16), v1, preferred_element_type=F32)
            o_acc[g][q_rows, :] = ov[:, :dh]
            m_acc[g][q_rows, :] = mx
            l_acc[g][q_rows, :] = ov[:, dh:dh + 1]
            return carry

        lax.fori_loop(0, span // steps, tile, 0, unroll=4)

    def merge(c, carry):
        sl = pl.ds(pl.multiple_of(c * steps, steps), steps)
        ms = [m[sl, :] for m in m_acc]
        top = functools.reduce(jnp.maximum, ms)
        num = jnp.zeros((steps, o_ref.shape[-1]), F32)
        den = jnp.zeros((steps, 1), F32)
        for g in range(n_pat):
            w = jnp.exp(ms[g] - top)
            num = num + w * o_acc[g][sl, :]
            den = den + w * l_acc[g][sl, :]
        o_ref[sl, :] = (num / den).astype(o_ref.dtype)
        return carry

    lax.fori_loop(0, span // steps, merge, 0, unroll=2)


def dilated_attention(qkv, n_heads, *, q_col, k_col, v_col):
    b, s, _ = qkv.shape
    dh = HEAD_DIM
    span, steps = A_SPAN, A_STEPS
    n_pat = len(A_PATTERNS)

    def cur(col):
        return pl.BlockSpec((None, span, dh), lambda bb, h, i: (bb, i, col + h))

    def prev(col):
        return pl.BlockSpec((None, span, dh), lambda bb, h, i: (bb, jnp.maximum(i - 1, 0), col + h))

    scratch = [pltpu.VMEM((2 * span, dh), F32), pltpu.VMEM((2 * span, dh), F32),
               pltpu.VMEM((2 * n_pat, steps, 2 * steps), F32)]
    scratch += [pltpu.VMEM((span, dh), F32) for _ in range(n_pat)]
    scratch += [pltpu.VMEM((span, 1), F32) for _ in range(2 * n_pat)]
    return pl.pallas_call(
        functools.partial(_dilated_kernel, span=span, steps=steps, patterns=A_PATTERNS, scale=dh ** -0.5),
        grid=(b, n_heads, s // span),
        in_specs=[pl.BlockSpec(memory_space=pltpu.SMEM),
                  cur(q_col), cur(k_col), prev(k_col), cur(v_col), prev(v_col)],
        out_specs=pl.BlockSpec((None, span, dh), lambda bb, h, i: (bb, i, h)),
        out_shape=jax.ShapeDtypeStruct((b, s, n_heads * dh), BF16),
        scratch_shapes=scratch,
        compiler_params=_cparams(("parallel", "parallel", "arbitrary")),
        name="dilated_attention",
    )(_alibi_slopes(n_heads), qkv, qkv, qkv, qkv, qkv)


def _conv4_kernel(x_ref, halo_ref, w_ref, o_ref, buf_ref, *, t_rows, width, k_blocks_from, k_scale):
    i = pl.program_id(1)
    j = pl.program_id(2)
    halo = jnp.where(i > 0, halo_ref[...], 0.0)
    buf_ref[0:SUBLANES, :] = halo
    buf_ref[SUBLANES:SUBLANES + t_rows, :] = x_ref[...]
    off = SUBLANES - (width - 1)
    acc = jnp.zeros(x_ref.shape, F32)
    for tap in range(width):
        acc = acc + w_ref[tap:tap + 1, :] * buf_ref[off + tap:off + tap + t_rows, :]
    y = acc * _sigmoid(acc)
    y = y * jnp.where(j >= k_blocks_from, k_scale, 1.0)
    o_ref[...] = y.astype(o_ref.dtype)


def conv4_silu(qkv, conv_w, *, col0, width_cols, k_scale, t_rows, tc):
    b, s, _ = qkv.shape
    width = conv_w.shape[0]
    cb0 = col0 // tc
    n_cb = width_cols // tc
    rb = t_rows // SUBLANES
    return pl.pallas_call(
        functools.partial(_conv4_kernel, t_rows=t_rows, width=width, k_blocks_from=n_cb // 2, k_scale=k_scale),
        grid=(b, s // t_rows, n_cb),
        in_specs=[pl.BlockSpec((None, t_rows, tc), lambda bb, i, j: (bb, i, cb0 + j)),
                  pl.BlockSpec((None, SUBLANES, tc), lambda bb, i, j: (bb, jnp.maximum(i * rb - 1, 0), cb0 + j)),
                  pl.BlockSpec((width, tc), lambda bb, i, j: (0, j))],
        out_specs=pl.BlockSpec((None, t_rows, tc), lambda bb, i, j: (bb, i, j)),
        out_shape=jax.ShapeDtypeStruct((b, s, width_cols), BF16),
        scratch_shapes=[pltpu.VMEM((SUBLANES + t_rows, tc), F32)],
        compiler_params=_cparams(("parallel", "parallel", "parallel")),
        name="conv4_silu",
    )(qkv, qkv, conv_w)


def _mlstm_kernel(big_ref, bfg_ref, gates_ref, q_ref, k_ref, v_ref, og_ref, ng_ref, o_ref,
                  c_ref, n_ref, m_ref, *, chunk):
    h = pl.program_id(1)
    c = pl.program_id(2)

    @pl.when(c == 0)
    def _init():
        c_ref[...] = jnp.zeros_like(c_ref)
        n_ref[...] = jnp.zeros_like(n_ref)
        m_ref[...] = jnp.zeros_like(m_ref)

    gates = gates_ref[...]
    i_row = gates[0:1, :] + big_ref[h]
    f_row = gates[1:2, :] + bfg_ref[h]
    lf_row = jnp.minimum(f_row, 0.0) - jnp.log(1.0 + jnp.exp(-jnp.abs(f_row)))

    tt = lax.broadcasted_iota(jnp.int32, (chunk, chunk), 0)
    ss = lax.broadcasted_iota(jnp.int32, (chunk, chunk), 1)
    causal = ss <= tt
    diag = ss == tt
    lf_b = jnp.broadcast_to(lf_row, (chunk, chunk))
    cum_col = jnp.sum(jnp.where(causal, lf_b, 0.0), axis=1, keepdims=True)
    lf_col = jnp.sum(jnp.where(diag, lf_b, 0.0), axis=1, keepdims=True)
    i_col = jnp.sum(jnp.where(diag, jnp.broadcast_to(i_row, (chunk, chunk)), 0.0), axis=1, keepdims=True)
    cum_row = jnp.sum(jnp.where(tt <= ss, jnp.broadcast_to(lf_col, (chunk, chunk)), 0.0),
                      axis=0, keepdims=True)
    cum_last = jnp.sum(lf_row, axis=1, keepdims=True)
    m_prev = m_ref[...]

    log_w = jnp.where(causal, cum_col - cum_row + i_row, NEG_INF)
    log_inter = cum_col + m_prev
    m_row = jnp.maximum(log_inter, jnp.max(log_w, axis=1, keepdims=True))
    w_intra = jnp.exp(log_w - m_row)
    w_inter = jnp.exp(log_inter - m_row)

    q = q_ref[...]
    k = k_ref[...]
    v = v_ref[...].astype(BF16)
    c_state = c_ref[...]
    n_state = n_ref[...]
    attn = w_intra * lax.dot_general(q, k, NT_DIMS, preferred_element_type=F32)
    num = (w_inter * jnp.dot(q, c_state.astype(BF16), preferred_element_type=F32)
           + jnp.dot(attn.astype(BF16), v, preferred_element_type=F32))
    den = (w_inter * jnp.sum(q.astype(F32) * n_state, axis=1, keepdims=True)
           + jnp.sum(attn, axis=1, keepdims=True))
    h_out = num / jnp.maximum(jnp.abs(den), jnp.exp(-m_row))

    lte_col = cum_last - cum_col + i_col
    lte_row = cum_last - cum_row + i_row
    m_new = jnp.maximum(cum_last + m_prev, jnp.max(lte_row, axis=1, keepdims=True))
    w_end = jnp.exp(lte_col - m_new)
    decay = jnp.exp(cum_last + m_prev - m_new)
    kw = k.astype(F32) * w_end
    c_ref[...] = decay * c_state + jnp.dot(kw.T.astype(BF16), v, preferred_element_type=F32)
    n_ref[...] = decay * n_state + jnp.sum(kw, axis=0, keepdims=True)
    m_ref[...] = m_new

    mu = jnp.mean(h_out, axis=1, keepdims=True)
    hc = h_out - mu
    var = jnp.mean(hc * hc, axis=1, keepdims=True)
    h_norm = hc * lax.rsqrt(var + LN_EPS) * ng_ref[...]
    o_ref[...] = (_sigmoid(og_ref[...]) * h_norm).astype(o_ref.dtype)


def mlstm(qkv, qk_conv, gates, b_igate, b_fgate, norm_g, *, v_col0, o_col0):
    b, s, _ = qkv.shape
    n_heads = gates.shape[1]
    dk = qk_conv.shape[-1] // (2 * n_heads)
    dv = norm_g.shape[-1] // n_heads
    chunk = B_CHUNK
    return pl.pallas_call(
        functools.partial(_mlstm_kernel, chunk=chunk),
        grid=(b, n_heads, s // chunk),
        in_specs=[pl.BlockSpec(memory_space=pltpu.SMEM),
                  pl.BlockSpec(memory_space=pltpu.SMEM),
                  pl.BlockSpec((None, None, 2, chunk), lambda bb, h, c: (bb, h, 0, c)),
                  pl.BlockSpec((None, chunk, dk), lambda bb, h, c: (bb, c, h)),
                  pl.BlockSpec((None, chunk, dk), lambda bb, h, c: (bb, c, n_heads + h)),
                  pl.BlockSpec((None, chunk, dv), lambda bb, h, c: (bb, c, v_col0 // dv + h)),
                  pl.BlockSpec((None, chunk, dv), lambda bb, h, c: (bb, c, o_col0 // dv + h)),
                  pl.BlockSpec((1, dv), lambda bb, h, c: (0, h))],
        out_specs=pl.BlockSpec((None, chunk, dv), lambda bb, h, c: (bb, c, h)),
        out_shape=jax.ShapeDtypeStruct((b, s, n_heads * dv), BF16),
        scratch_shapes=[pltpu.VMEM((dk, dv), F32), pltpu.VMEM((1, dk), F32), pltpu.VMEM((1, 1), F32)],
        compiler_params=_cparams(("parallel", "parallel", "arbitrary")),
        name="mlstm",
    )(b_igate, b_fgate, gates, qk_conv, qk_conv, qkv, qkv, norm_g.reshape(1, -1))


MOBA_PENALTY = -3.0e38
MOBA_BIAS_PARTS = 3
MOBA_GROUP = 4
MOBA_KEY_CHUNK = 1024


def _moba_kernel(slopes_ref, q_ref, k_ref, v_ref, o_ref, kaug_ref, vaug_ref, kmean_ref, pen_ref,
                 *, blk, n_blk, top_k, scale):
    h = pl.program_id(1)
    slope = slopes_ref[h]
    dh = q_ref.shape[-1]
    lane = lax.broadcasted_iota(jnp.int32, (blk, LANES), 1)
    row = lax.broadcasted_iota(jnp.int32, (blk, 1), 0)

    kmean_ref[...] = jnp.zeros_like(kmean_ref)
    ones_col = jnp.where(lane == 0, 1.0, 0.0).astype(BF16)
    for n in range(n_blk):
        rows = slice(n * blk, (n + 1) * blk)
        kb = k_ref[rows, :]
        kmean_ref[n:n + 1, :] = jnp.mean(kb, axis=0, keepdims=True)
        aug = jnp.where(lane == n, 1.0, 0.0)
        rem = slope * (row + n * blk).astype(F32)
        for part in range(MOBA_BIAS_PARTS):
            piece = rem.astype(BF16).astype(F32)
            aug = aug + jnp.where(lane == n_blk + part, piece, 0.0)
            rem = rem - piece
        kaug_ref[rows, 0:dh] = kb.astype(BF16)
        kaug_ref[rows, dh:2 * dh] = aug.astype(BF16)
        vaug_ref[rows, 0:dh] = v_ref[rows, :].astype(BF16)
        vaug_ref[rows, dh:2 * dh] = ones_col

    ri = lax.broadcasted_iota(jnp.int32, (blk, blk), 0)
    cj = lax.broadcasted_iota(jnp.int32, (blk, blk), 1)

    blk_id = lax.broadcasted_iota(jnp.int32, (n_blk, blk), 0)
    tail_id = lax.broadcasted_iota(jnp.int32, (LANES - n_blk, blk), 0)
    tail = jnp.where(tail_id < MOBA_BIAS_PARTS, 1.0, 0.0)

    def select(qb, carry):
        q_rows = pl.ds(pl.multiple_of(qb * blk, blk), blk)
        gate = lax.dot_general(kmean_ref[0:n_blk, :], q_ref[q_rows, :] * scale, NT_DIMS,
                               precision=lax.Precision.HIGHEST, preferred_element_type=F32)
        gate = jnp.where(blk_id < qb, gate, NEG_INF)
        sel = jnp.zeros(gate.shape, F32)
        for j in range(top_k):
            mx = jnp.max(gate, axis=0, keepdims=True)
            idx = jnp.min(jnp.where(gate == mx, blk_id, n_blk), axis=0, keepdims=True)
            pick = blk_id == idx
            sel = sel + jnp.where(pick, jnp.where(qb > j, 1.0, 0.0), 0.0)
            gate = jnp.where(pick, NEG_INF, gate)
        pen_t = jnp.concatenate([jnp.where(sel > 0.0, 0.0, MOBA_PENALTY), tail], axis=0)
        pen_ref[qb] = pen_t.T.astype(BF16)
        return carry

    lax.fori_loop(0, n_blk, select, 0, unroll=MOBA_GROUP)

    def make_attend(n_keys):
        def attend(qb, carry):
            q_rows = pl.ds(pl.multiple_of(qb * blk, blk), blk)
            q16 = (q_ref[q_rows, :] * scale).astype(BF16)
            pen = pen_ref[qb]
            q_past = jnp.concatenate([q16, pen], axis=1)
            q_own = jnp.concatenate([q16, jnp.where(lane == qb, 0.0, pen.astype(F32)).astype(BF16)], axis=1)
            s_own = lax.dot_general(q_own, kaug_ref[q_rows, :], NT_DIMS, preferred_element_type=F32)
            s_own = jnp.where(ri >= cj, s_own, NEG_INF)
            m = jnp.max(s_own, axis=1, keepdims=True)
            acc = jnp.dot(jnp.exp(s_own - m).astype(BF16), vaug_ref[q_rows, :], preferred_element_type=F32)
            for k0 in range(0, n_keys, MOBA_KEY_CHUNK):
                keys = slice(k0, min(k0 + MOBA_KEY_CHUNK, n_keys))
                s = lax.dot_general(q_past, kaug_ref[keys, :], NT_DIMS, preferred_element_type=F32)
                m_new = jnp.maximum(m, jnp.max(s, axis=1, keepdims=True))
                acc = (jnp.exp(m - m_new) * acc
                       + jnp.dot(jnp.exp(s - m_new).astype(BF16), vaug_ref[keys, :], preferred_element_type=F32))
                m = m_new
            o_ref[q_rows, :] = (acc[:, :dh] / acc[:, dh:dh + 1]).astype(o_ref.dtype)
            return carry
        return attend

    for first in range(0, n_blk, MOBA_GROUP):
        last = min(first + MOBA_GROUP, n_blk)
        lax.fori_loop(first, last, make_attend(last * blk), 0, unroll=2)


def moba_attention(qkv, n_heads, *, q_col, k_col, v_col):
    b, s, _ = qkv.shape
    dh = HEAD_DIM
    blk = MOBA_BLOCK
    n_blk = s // blk
    top_k = min(MOBA_TOPK, n_blk)
    assert n_blk + MOBA_BIAS_PARTS <= LANES and n_blk % SUBLANES == 0 and dh == LANES

    def head(col):
        return pl.BlockSpec((None, s, dh), lambda bb, h: (bb, 0, col + h))

    return pl.pallas_call(
        functools.partial(_moba_kernel, blk=blk, n_blk=n_blk, top_k=top_k, scale=dh ** -0.5),
        grid=(b, n_heads),
        in_specs=[pl.BlockSpec(memory_space=pltpu.SMEM), head(q_col), head(k_col), head(v_col)],
        out_specs=pl.BlockSpec((None, s, dh), lambda bb, h: (bb, 0, h)),
        out_shape=jax.ShapeDtypeStruct((b, s, n_heads * dh), BF16),
        scratch_shapes=[pltpu.VMEM((s, 2 * dh), BF16), pltpu.VMEM((s, 2 * dh), BF16),
                        pltpu.VMEM((LANES, dh), F32), pltpu.VMEM((n_blk, blk, LANES), BF16)],
        compiler_params=_cparams(("parallel", "parallel")),
        name="moba_attention",
    )(_alibi_slopes(n_heads), qkv, qkv, qkv)


CONV_HALO = 32
LN_ROWS = 16


def _conv31_kernel(a_ref, ah_ref, b_ref, bh_ref, w_ref, g_ref, beta_ref, o_ref, glu_ref, conv_ref, shift_ref,
                   *, t_rows, width, channels):
    i = pl.program_id(1)
    halo = ah_ref[...] * _sigmoid(bh_ref[...])
    glu_ref[0:CONV_HALO, :] = jnp.where(i > 0, halo, 0.0)
    glu_ref[CONV_HALO:CONV_HALO + t_rows, :] = a_ref[...] * _sigmoid(b_ref[...])
    off = CONV_HALO - (width - 1)

    strip_rows = CONV_HALO + t_rows - SUBLANES

    def cols(c, carry):
        cs = pl.ds(pl.multiple_of(c * LANES, LANES), LANES)
        for phase in range(1, SUBLANES):
            shift_ref[phase, 0:strip_rows, :] = glu_ref[phase:phase + strip_rows, cs]
        acc = jnp.zeros((t_rows, LANES), F32)
        for tap in range(width):
            phase, base = (off + tap) % SUBLANES, (off + tap) // SUBLANES * SUBLANES
            src = glu_ref[base:base + t_rows, cs] if phase == 0 else shift_ref[phase, base:base + t_rows, :]
            acc = acc + w_ref[tap:tap + 1, cs] * src
        conv_ref[:, cs] = acc
        return carry

    lax.fori_loop(0, channels // LANES, cols, 0)

    def rows(r, carry):
        rs = pl.ds(pl.multiple_of(r * LN_ROWS, LN_ROWS), LN_ROWS)
        y = _ln_rows(conv_ref[rs, :], g_ref[...], beta_ref[...])
        o_ref[rs, :] = (y * _sigmoid(y)).astype(o_ref.dtype)
        return carry

    lax.fori_loop(0, t_rows // LN_ROWS, rows, 0, unroll=2)


def glu_conv_ln_silu(qkv, conv_w, ln_g, ln_b, *, a_col0, b_col0, channels, t_rows):
    b, s, _ = qkv.shape
    width = conv_w.shape[0]
    hb = t_rows // CONV_HALO
    ca, cb = a_col0 // channels, b_col0 // channels

    def cur(col):
        return pl.BlockSpec((None, t_rows, channels), lambda bb, i: (bb, i, col))

    def halo(col):
        return pl.BlockSpec((None, CONV_HALO, channels), lambda bb, i: (bb, jnp.maximum(i * hb - 1, 0), col))

    vec = pl.BlockSpec((1, channels), lambda bb, i: (0, 0))
    return pl.pallas_call(
        functools.partial(_conv31_kernel, t_rows=t_rows, width=width, channels=channels),
        grid=(b, s // t_rows),
        in_specs=[cur(ca), halo(ca), cur(cb), halo(cb),
                  pl.BlockSpec((width, channels), lambda bb, i: (0, 0)), vec, vec],
        out_specs=pl.BlockSpec((None, t_rows, channels), lambda bb, i: (bb, i, 0)),
        out_shape=jax.ShapeDtypeStruct((b, s, channels), BF16),
        scratch_shapes=[pltpu.VMEM((CONV_HALO + t_rows, channels), F32), pltpu.VMEM((t_rows, channels), F32),
                        pltpu.VMEM((SUBLANES, CONV_HALO + t_rows - SUBLANES, LANES), F32)],
        compiler_params=_cparams(("parallel", "parallel")),
        name="glu_conv_ln_silu",
    )(qkv, qkv, qkv, qkv, conv_w, ln_g.reshape(1, -1), ln_b.reshape(1, -1))


DOWN_TILES = dict(tm=2048, tn=1024, tk=1024)
PANEL_TILES = dict(tm=2048, tn=256)
ROW_TILE = 256


def _ffn(u, w_up, w_down):
    hmid = matmul_swiglu(u, w_up, **PANEL_TILES)
    return matmul_f32(hmid, w_down, w_down.shape[1], **DOWN_TILES)


def _mixer_dilated_mlstm(u, bsz, seq, w_in, conv_qk, b_igate, b_fgate, norm_g, w_out):
    d = u.shape[1]
    a_width = d // 2
    a_heads = a_width // HEAD_DIM
    b_width = d - a_width
    qk_width = b_width // 2
    n_main = 3 * a_width + 2 * qk_width + 2 * b_width
    w_in_t = w_in.T
    proj = matmul_rows([u], w_in_t, n_main, w_is_transposed=True, **PANEL_TILES).reshape(bsz, seq, n_main)
    gates = matmul_rows([u], w_in_t, LANES, tm=PANEL_TILES["tm"], tn=LANES, col_block0=n_main // LANES,
                        single_buffer_a=True, w_is_transposed=True)
    gates = gates[:, :2 * B_HEADS].reshape(bsz, seq, 2, B_HEADS).transpose(0, 3, 2, 1)

    hb = a_width // HEAD_DIM
    a_out = dilated_attention(proj, a_heads, q_col=0, k_col=hb, v_col=2 * hb)
    qk_col0 = 3 * a_width
    qk_conv = conv4_silu(proj, conv_qk, col0=qk_col0, width_cols=2 * qk_width,
                         k_scale=(qk_width // B_HEADS) ** -0.5, t_rows=512, tc=512)
    b_out = mlstm(proj, qk_conv, gates, b_igate, b_fgate, norm_g,
                  v_col0=qk_col0 + 2 * qk_width, o_col0=qk_col0 + 2 * qk_width + b_width)
    halves = [t.reshape(bsz * seq, -1) for t in (a_out, b_out)]
    return matmul_rows(halves, w_out, d, **PANEL_TILES)


def _mixer_moba_conv(u, bsz, seq, w_in, conv_dw, conv_ln_g, conv_ln_b, w_out):
    d = u.shape[1]
    c_width = d // 2
    c_heads = c_width // HEAD_DIM
    d_ch = d - c_width
    n_main = 3 * c_width + 2 * d_ch
    proj = matmul_rows([u], w_in, n_main, **PANEL_TILES).reshape(bsz, seq, n_main)
    c_out = moba_attention(proj, c_heads, q_col=0, k_col=c_heads, v_col=2 * c_heads)
    d_out = glu_conv_ln_silu(proj, conv_dw, conv_ln_g, conv_ln_b, a_col0=3 * c_width,
                             b_col0=3 * c_width + d_ch, channels=d_ch, t_rows=ROW_TILE)
    halves = [t.reshape(bsz * seq, -1) for t in (c_out, d_out)]
    return matmul_rows(halves, w_out, d, **PANEL_TILES)


def kernel(x, c, l0_w_ada, l0_b_ada, l0_ln_g, l0_ln_b, l0_ffn1_w_up, l0_ffn1_w_down, l0_ffn2_w_up, l0_ffn2_w_down, l0_w_in, l0_conv_qk, l0_b_igate, l0_b_fgate, l0_mlstm_norm_g, l0_w_out, l1_w_ada, l1_b_ada, l1_ln_g, l1_ln_b, l1_ffn1_w_up, l1_ffn1_w_down, l1_ffn2_w_up, l1_ffn2_w_down, l1_w_in, l1_conv_dw, l1_conv_ln_g, l1_conv_ln_b, l1_w_out):
    bsz, seq, d = x.shape
    depth = 2
    alpha = (2 * depth) ** 0.25
    layers = (
        (l0_w_ada, l0_b_ada, l0_ln_g, l0_ln_b, l0_ffn1_w_up, l0_ffn1_w_down, l0_ffn2_w_up, l0_ffn2_w_down),
        (l1_w_ada, l1_b_ada, l1_ln_g, l1_ln_b, l1_ffn1_w_up, l1_ffn1_w_down, l1_ffn2_w_up, l1_ffn2_w_down),
    )
    mixers = (
        lambda u: _mixer_dilated_mlstm(u, bsz, seq, l0_w_in, l0_conv_qk, l0_b_igate, l0_b_fgate,
                                       l0_mlstm_norm_g, l0_w_out),
        lambda u: _mixer_moba_conv(u, bsz, seq, l1_w_in, l1_conv_dw, l1_conv_ln_g, l1_conv_ln_b, l1_w_out),
    )
    c_pad = jnp.pad(c, ((0, SUBLANES - bsz), (0, 0)))
    mods = [ada_modulation(c_pad, p[0], p[1], tn=512)[:bsz].reshape(bsz, 3 * N_SUBLAYERS, 1, d) for p in layers]

    x2 = x.reshape(bsz * seq, d)
    u = modulate(x2, mods[0], 0, 1, rows_per_batch=seq, tm=ROW_TILE)
    for layer in range(depth):
        _, _, ln_g, ln_b, f1_up, f1_down, f2_up, f2_down = layers[layer]
        branches = (
            (0.5, lambda v: _ffn(v, f1_up, f1_down)),
            (1.0, mixers[layer]),
            (0.5, lambda v: _ffn(v, f2_up, f2_down)),
        )
        for j, (res_w, fn) in enumerate(branches):
            y = fn(u)
            if j + 1 < N_SUBLAYERS:
                nxt, nj = mods[layer], (3 * (j + 1), 3 * (j + 1) + 1)
            elif layer + 1 < depth:
                nxt, nj = mods[layer + 1], (0, 1)
            else:
                nxt, nj = None, None
            x2, u = ln_residual(x2, y, mods[layer], 3 * j + 2, ln_g, ln_b, j, res_w, alpha, nxt, nj,
                                rows_per_batch=seq, tm=ROW_TILE)
    return x2.reshape(bsz, seq, d)
```

```python
import functools

import jax
import jax.numpy as jnp
import numpy as np
from jax import lax
from jax.experimental import pallas as pl
from jax.experimental.pallas import tpu as pltpu

F32 = jnp.float32
BF16 = jnp.bfloat16
NEG_INF = float("-inf")

LANES = 128
SUBLANES = 8
VMEM_LIMIT_BYTES = 56 * 1024 * 1024
SWIGLU_VMEM_LIMIT_BYTES = 60 * 1024 * 1024

HEAD_DIM = 128
A_PATTERNS = ((128, 1), (512, 4), (2048, 16))
A_STEPS = 128
A_SPAN = 2048
B_HEADS = 4
B_CONV_WIDTH = 4
B_CHUNK = 256
MOBA_BLOCK = 256
MOBA_TOPK = 3
D_CONV_WIDTH = 31
N_SUBLAYERS = 3
LN_EPS = 1e-5

NT_DIMS = (((1,), (1,)), ((), ()))


def _cparams(semantics, vmem_limit_bytes=VMEM_LIMIT_BYTES):
    return pltpu.CompilerParams(dimension_semantics=semantics, vmem_limit_bytes=vmem_limit_bytes)


def _sigmoid(x):
    return 1.0 / (1.0 + jnp.exp(-x))


def _alibi_slopes(n_heads):
    return jnp.asarray(2.0 ** (-8.0 * np.arange(1, n_heads + 1) / n_heads), F32)


def _mm_f32_kernel(a_ref, w_ref, o_ref):
    @pl.when(pl.program_id(2) == 0)
    def _init():
        o_ref[...] = jnp.zeros_like(o_ref)

    o_ref[...] += jnp.dot(a_ref[...], w_ref[...].astype(BF16), preferred_element_type=F32)


def matmul_f32(a, w, n_out, *, tm, tn, tk):
    m, k = a.shape
    grid = (m // tm, n_out // tn, k // tk)
    return pl.pallas_call(
        _mm_f32_kernel,
        grid=grid,
        in_specs=[pl.BlockSpec((tm, tk), lambda i, j, kk: (i, kk)),
                  pl.BlockSpec((tk, tn), lambda i, j, kk: (kk, j))],
        out_specs=pl.BlockSpec((tm, tn), lambda i, j, kk: (i, j)),
        out_shape=jax.ShapeDtypeStruct((m, n_out), F32),
        compiler_params=_cparams(("parallel", "parallel", "arbitrary")),
        name="matmul_f32",
    )(a, w)


def _mm_rows_kernel(*refs, n_a, w_is_transposed):
    a_refs, w_ref, o_ref = refs[:n_a], refs[n_a], refs[n_a + 1]
    acc, off = None, 0
    for a_ref in a_refs:
        ka = a_ref.shape[1]
        if w_is_transposed:
            part = lax.dot_general(a_ref[...], w_ref[:, off:off + ka].astype(BF16), NT_DIMS,
                                   preferred_element_type=F32)
        else:
            part = jnp.dot(a_ref[...], w_ref[off:off + ka, :].astype(BF16), preferred_element_type=F32)
        acc = part if acc is None else acc + part
        off += ka
    o_ref[...] = acc.astype(o_ref.dtype)


def matmul_rows(a_list, w, n_out, *, tm, tn, col_block0=0, single_buffer_a=False, w_is_transposed=False):
    m = a_list[0].shape[0]
    k = w.shape[1] if w_is_transposed else w.shape[0]
    assert sum(a.shape[1] for a in a_list) == k
    mode = dict(pipeline_mode=pl.Buffered(1)) if single_buffer_a else {}
    in_specs = [pl.BlockSpec((tm, a.shape[1]), lambda i, j: (i, 0), **mode) for a in a_list]
    if w_is_transposed:
        in_specs.append(pl.BlockSpec((tn, k), lambda i, j: (j + col_block0, 0)))
    else:
        in_specs.append(pl.BlockSpec((k, tn), lambda i, j: (0, j + col_block0)))
    return pl.pallas_call(
        functools.partial(_mm_rows_kernel, n_a=len(a_list), w_is_transposed=w_is_transposed),
        grid=(m // tm, pl.cdiv(n_out, tn)),
        in_specs=in_specs,
        out_specs=pl.BlockSpec((tm, tn), lambda i, j: (i, j)),
        out_shape=jax.ShapeDtypeStruct((m, n_out), F32),
        compiler_params=_cparams(("parallel", "arbitrary")),
        name="matmul_rows",
    )(*a_list, w)


def _mm_few_cols_kernel(a_ref, w_ref, o_ref):
    n_valid, k = w_ref.shape
    w = jnp.concatenate([w_ref[...], jnp.zeros((o_ref.shape[1] - n_valid, k), F32)], axis=0)
    o_ref[...] = lax.dot_general(a_ref[...], w.astype(BF16), NT_DIMS, preferred_element_type=F32)


def matmul_few_cols(a, w_t, row0, n_cols, *, tm):
    m, k = a.shape
    assert n_cols % SUBLANES == 0 and row0 % n_cols == 0
    return pl.pallas_call(
        _mm_few_cols_kernel,
        grid=(m // tm,),
        in_specs=[pl.BlockSpec((tm, k), lambda i: (i, 0)),
                  pl.BlockSpec((n_cols, k), lambda i: (row0 // n_cols, 0))],
        out_specs=pl.BlockSpec((tm, LANES), lambda i: (i, 0)),
        out_shape=jax.ShapeDtypeStruct((m, LANES), F32),
        compiler_params=_cparams(("parallel",)),
        name="matmul_few_cols",
    )(a, w_t)


def _mm_swiglu_kernel(a_ref, wg_ref, wv_ref, o_ref):
    a = a_ref[...]
    g = jnp.dot(a, wg_ref[...].astype(BF16), preferred_element_type=F32)
    v = jnp.dot(a, wv_ref[...].astype(BF16), preferred_element_type=F32)
    o_ref[...] = (g * _sigmoid(g) * v).astype(o_ref.dtype)


def matmul_swiglu(a, w_up, *, tm, tn):
    m, k = a.shape
    f = w_up.shape[1] // 2
    nh = f // tn
    return pl.pallas_call(
        _mm_swiglu_kernel,
        grid=(m // tm, nh),
        in_specs=[pl.BlockSpec((tm, k), lambda i, j: (i, 0)),
                  pl.BlockSpec((k, tn), lambda i, j: (0, j)),
                  pl.BlockSpec((k, tn), lambda i, j: (0, j + nh))],
        out_specs=pl.BlockSpec((tm, tn), lambda i, j: (i, j)),
        out_shape=jax.ShapeDtypeStruct((m, f), BF16),
        compiler_params=_cparams(("parallel", "arbitrary"), SWIGLU_VMEM_LIMIT_BYTES),
        name="matmul_swiglu",
    )(a, w_up, w_up)


def _ada_kernel(c_ref, w_ref, b_ref, o_ref):
    c = c_ref[...]
    a = (c * _sigmoid(c)).astype(BF16)
    o_ref[...] = jnp.dot(a, w_ref[...].astype(BF16), preferred_element_type=F32) + b_ref[...]


def ada_modulation(c_pad, w_ada, b_ada, *, tn):
    rows, d = c_pad.shape
    n = w_ada.shape[1]
    return pl.pallas_call(
        _ada_kernel,
        grid=(n // tn,),
        in_specs=[pl.BlockSpec((rows, d), lambda j: (0, 0)),
                  pl.BlockSpec((d, tn), lambda j: (0, j)),
                  pl.BlockSpec((1, tn), lambda j: (0, j))],
        out_specs=pl.BlockSpec((rows, tn), lambda j: (0, j)),
        out_shape=jax.ShapeDtypeStruct((rows, n), F32),
        compiler_params=_cparams(("parallel",)),
        name="ada_modulation",
    )(c_pad, w_ada, b_ada.reshape(1, n))


def _modulate_kernel(x_ref, sc_ref, sh_ref, u_ref):
    u_ref[...] = (x_ref[...] * (1.0 + sc_ref[...]) + sh_ref[...]).astype(u_ref.dtype)


def modulate(x2, mod, j_shift, j_scale, *, rows_per_batch, tm):
    m, d = x2.shape
    bpb = rows_per_batch // tm

    def mspec(j):
        return pl.BlockSpec((None, None, 1, d), lambda i: (i // bpb, j, 0, 0))

    return pl.pallas_call(
        _modulate_kernel,
        grid=(m // tm,),
        in_specs=[pl.BlockSpec((tm, d), lambda i: (i, 0)), mspec(j_scale), mspec(j_shift)],
        out_specs=pl.BlockSpec((tm, d), lambda i: (i, 0)),
        out_shape=jax.ShapeDtypeStruct((m, d), BF16),
        compiler_params=_cparams(("parallel",)),
        name="modulate",
    )(x2, mod, mod)


def _ln_rows(z, g, b):
    mu = jnp.mean(z, axis=-1, keepdims=True)
    zc = z - mu
    var = jnp.mean(zc * zc, axis=-1, keepdims=True)
    return zc * lax.rsqrt(var + LN_EPS) * g + b


def _ln_res_kernel(x_ref, y_ref, gate_ref, g_ref, b_ref, *rest, alpha, res_w, tm, with_u):
    tile = (SUBLANES, x_ref.shape[-1])
    if with_u:
        sc_ref, sh_ref, xo_ref, uo_ref, gw_ref, g1_ref, b1_ref, g2_ref, b2_ref = rest
        g2_ref[...] = jnp.broadcast_to(g_ref[...] * (1.0 + sc_ref[...]), tile)
        b2_ref[...] = jnp.broadcast_to(b_ref[...] * (1.0 + sc_ref[...]) + sh_ref[...], tile)
    else:
        xo_ref, gw_ref, g1_ref, b1_ref = rest
    gw_ref[...] = jnp.broadcast_to(res_w * (1.0 + gate_ref[...]), tile)
    g1_ref[...] = jnp.broadcast_to(g_ref[...], tile)
    b1_ref[...] = jnp.broadcast_to(b_ref[...], tile)
    inv_d = 1.0 / x_ref.shape[-1]

    def rows(r, carry):
        sl = pl.ds(pl.multiple_of(r * SUBLANES, SUBLANES), SUBLANES)
        z = alpha * x_ref[sl, :] + gw_ref[...] * y_ref[sl, :]
        mu = jnp.sum(z, axis=-1, keepdims=True) * inv_d
        zc = z - mu
        var = jnp.sum(zc * zc, axis=-1, keepdims=True) * inv_d
        t = zc * lax.rsqrt(var + LN_EPS)
        xo_ref[sl, :] = t * g1_ref[...] + b1_ref[...]
        if with_u:
            uo_ref[sl, :] = (t * g2_ref[...] + b2_ref[...]).astype(uo_ref.dtype)
        return carry

    lax.fori_loop(0, tm // SUBLANES, rows, 0, unroll=4)


def ln_residual(x2, y2, mod, j_gate, ln_g, ln_b, j_ln, res_w, alpha, next_mod, next_j, *, rows_per_batch, tm):
    m, d = x2.shape
    bpb = rows_per_batch // tm
    with_u = next_mod is not None

    def mspec(j):
        return pl.BlockSpec((None, None, 1, d), lambda i: (i // bpb, j, 0, 0))

    row_spec = pl.BlockSpec((tm, d), lambda i: (i, 0))
    ln_spec = pl.BlockSpec((None, 1, d), lambda i: (j_ln, 0, 0))
    in_specs = [row_spec, row_spec, mspec(j_gate), ln_spec, ln_spec]
    args = [x2, y2, mod, ln_g.reshape(N_SUBLAYERS, 1, d), ln_b.reshape(N_SUBLAYERS, 1, d)]
    out_specs = [row_spec]
    out_shape = [jax.ShapeDtypeStruct((m, d), F32)]
    if with_u:
        j_shift, j_scale = next_j
        in_specs += [mspec(j_scale), mspec(j_shift)]
        args += [next_mod, next_mod]
        out_specs.append(row_spec)
        out_shape.append(jax.ShapeDtypeStruct((m, d), BF16))
    outs = pl.pallas_call(
        functools.partial(_ln_res_kernel, alpha=alpha, res_w=res_w, tm=tm, with_u=with_u),
        grid=(m // tm,),
        in_specs=in_specs,
        out_specs=out_specs,
        out_shape=out_shape,
        scratch_shapes=[pltpu.VMEM((SUBLANES, d), F32)] * (5 if with_u else 3),
        compiler_params=_cparams(("parallel",)),
        name="ln_residual",
    )(*args)
    return (outs[0], outs[1]) if with_u else (outs[0], None)


A_SPLIT = 4


def _dilated_kernel(slopes_ref, q_ref, kc_ref, kp_ref, vc_ref, vp_ref, o_ref,
                    k4_ref, v4_ref, bias_ref, *acc_refs, span, steps, patterns, scale):
    h = pl.program_id(1)
    i = pl.program_id(2)
    slope = slopes_ref[h]
    n_pat = len(patterns)
    o_acc = acc_refs[:n_pat]
    m_acc = acc_refs[n_pat:2 * n_pat]
    l_acc = acc_refs[2 * n_pat:]
    dh = q_ref.shape[-1]
    part = span // A_SPLIT

    for c in range(A_SPLIT):
        for dst, prev_ref, cur_ref in ((k4_ref, kp_ref, kc_ref), (v4_ref, vp_ref, vc_ref)):
            dst[c, 0:part, :] = prev_ref[pl.ds(c, part, stride=A_SPLIT), :]
            dst[c, part:2 * part, :] = cur_ref[pl.ds(c, part, stride=A_SPLIT), :]

    qi = lax.broadcasted_iota(jnp.int32, (steps, 2 * steps), 0)
    kj = lax.broadcasted_iota(jnp.int32, (steps, 2 * steps), 1)
    back = steps + qi - kj
    in_band = (back >= 0) & (back <= steps)
    backf = back.astype(F32)
    for g, (_, dil) in enumerate(patterns):
        bias = jnp.where(in_band, (-(slope * dil)) * backf, NEG_INF)
        bias_ref[2 * g] = bias
        bias_ref[2 * g + 1] = jnp.where(kj >= steps, bias, NEG_INF)
    ones = jnp.ones((2 * steps, dh), BF16)

    first_span = jnp.where(i > 0, 0, 1)

    def tile(g, dil, r, n):
        q_start = r + dil * steps * n
        q_rows = pl.ds(q_start, steps, stride=dil)
        q = q_ref[q_rows, :]
        if dil % A_SPLIT == 0:
            c, sub = r % A_SPLIT, dil // A_SPLIT
            k_rows = pl.ds((span + q_start - c - dil * steps) // A_SPLIT, 2 * steps, stride=sub)
            k, v = k4_ref[c, k_rows, :], v4_ref[c, k_rows, :]
        elif n == 0:
            k = jnp.concatenate([kp_ref[span - steps:span, :], kc_ref[0:steps, :]], axis=0)
            v = jnp.concatenate([vp_ref[span - steps:span, :], vc_ref[0:steps, :]], axis=0)
        else:
            k, v = kc_ref[q_start - steps:q_start + steps, :], vc_ref[q_start - steps:q_start + steps, :]
        v_ones = jnp.concatenate([v.astype(BF16), ones], axis=1)
        bias = bias_ref[2 * g] if n > 0 else bias_ref[2 * g + first_span]
        s = lax.dot_general((q * scale).astype(BF16), k.astype(BF16), NT_DIMS, preferred_element_type=F32) + bias
        mx = jnp.max(s, axis=1, keepdims=True)
        p = jnp.exp(s - mx)
        ov = jnp.dot(p.astype(BF16), v_ones, preferred_element_type=F32)
        o_acc[g][q_rows, :] = ov[:, :dh]
        m_acc[g][q_rows, :] = jnp.broadcast_to(mx, (steps, dh))
        l_acc[g][q_rows, :] = ov[:, dh:]

    for g, (_, dil) in enumerate(patterns):
        for t in range(span // steps):
            tile(g, dil, t % dil, t // dil)

    def merge(c, carry):
        sl = pl.ds(pl.multiple_of(c * steps, steps), steps)
        ms = [m[sl, :] for m in m_acc]
        top = functools.reduce(jnp.maximum, ms)
        num = jnp.zeros((steps, dh), F32)
        den = jnp.zeros((steps, dh), F32)
        for g in range(n_pat):
            w = jnp.exp(ms[g] - top)
            num = num + w * o_acc[g][sl, :]
            den = den + w * l_acc[g][sl, :]
        o_ref[sl, :] = (num / den).astype(o_ref.dtype)
        return carry

    lax.fori_loop(0, span // steps, merge, 0, unroll=2)


def dilated_attention(qkv, n_heads, *, q_col, k_col, v_col):
    b, s, _ = qkv.shape
    dh = HEAD_DIM
    span, steps = A_SPAN, A_STEPS
    n_pat = len(A_PATTERNS)

    def cur(col):
        return pl.BlockSpec((None, span, dh), lambda bb, h, i: (bb, i, col + h))

    def prev(col):
        return pl.BlockSpec((None, span, dh), lambda bb, h, i: (bb, jnp.maximum(i - 1, 0), col + h))

    assert all(dil == 1 or dil % A_SPLIT == 0 for _, dil in A_PATTERNS)
    part = span // A_SPLIT
    scratch = [pltpu.VMEM((A_SPLIT, 2 * part, dh), F32), pltpu.VMEM((A_SPLIT, 2 * part, dh), F32),
               pltpu.VMEM((2 * n_pat, steps, 2 * steps), F32)]
    scratch += [pltpu.VMEM((span, dh), F32) for _ in range(3 * n_pat)]
    return pl.pallas_call(
        functools.partial(_dilated_kernel, span=span, steps=steps, patterns=A_PATTERNS, scale=dh ** -0.5),
        grid=(b, n_heads, s // span),
        in_specs=[pl.BlockSpec(memory_space=pltpu.SMEM),
                  cur(q_col), cur(k_col), prev(k_col), cur(v_col), prev(v_col)],
        out_specs=pl.BlockSpec((None, span, dh), lambda bb, h, i: (bb, i, h)),
        out_shape=jax.ShapeDtypeStruct((b, s, n_heads * dh), BF16),
        scratch_shapes=scratch,
        compiler_params=_cparams(("parallel", "parallel", "arbitrary")),
        name="dilated_attention",
    )(_alibi_slopes(n_heads), qkv, qkv, qkv, qkv, qkv)


def _conv4_kernel(x_ref, halo_ref, w_ref, o_ref, buf_ref, *, t_rows, width, k_blocks_from, k_scale):
    i = pl.program_id(1)
    j = pl.program_id(2)
    halo = jnp.where(i > 0, halo_ref[...], 0.0)
    buf_ref[0:SUBLANES, :] = halo
    buf_ref[SUBLANES:SUBLANES + t_rows, :] = x_ref[...]
    off = SUBLANES - (width - 1)
    acc = jnp.zeros(x_ref.shape, F32)
    for tap in range(width):
        acc = acc + w_ref[tap:tap + 1, :] * buf_ref[off + tap:off + tap + t_rows, :]
    y = acc * _sigmoid(acc)
    y = y * jnp.where(j >= k_blocks_from, k_scale, 1.0)
    o_ref[...] = y.astype(o_ref.dtype)


def conv4_silu(qkv, conv_w, *, col0, width_cols, k_scale, t_rows, tc):
    b, s, _ = qkv.shape
    width = conv_w.shape[0]
    cb0 = col0 // tc
    n_cb = width_cols // tc
    rb = t_rows // SUBLANES
    return pl.pallas_call(
        functools.partial(_conv4_kernel, t_rows=t_rows, width=width, k_blocks_from=n_cb // 2, k_scale=k_scale),
        grid=(b, s // t_rows, n_cb),
        in_specs=[pl.BlockSpec((None, t_rows, tc), lambda bb, i, j: (bb, i, cb0 + j)),
                  pl.BlockSpec((None, SUBLANES, tc), lambda bb, i, j: (bb, jnp.maximum(i * rb - 1, 0), cb0 + j)),
                  pl.BlockSpec((width, tc), lambda bb, i, j: (0, j))],
        out_specs=pl.BlockSpec((None, t_rows, tc), lambda bb, i, j: (bb, i, j)),
        out_shape=jax.ShapeDtypeStruct((b, s, width_cols), BF16),
        scratch_shapes=[pltpu.VMEM((SUBLANES + t_rows, tc), F32)],
        compiler_params=_cparams(("parallel", "parallel", "parallel")),
        name="conv4_silu",
    )(qkv, qkv, conv_w)


def _mlstm_kernel(big_ref, bfg_ref, gates_ref, q_ref, k_ref, v_ref, og_ref, ng_ref, o_ref,
                  c_ref, n_ref, m_ref, *, chunk):
    h = pl.program_id(1)
    c = pl.program_id(2)

    @pl.when(c == 0)
    def _init():
        c_ref[...] = jnp.zeros_like(c_ref)
        n_ref[...] = jnp.zeros_like(n_ref)
        m_ref[...] = jnp.zeros_like(m_ref)

    gates = gates_ref[...]
    i_row = gates[0:1, :] + big_ref[h]
    f_row = gates[1:2, :] + bfg_ref[h]
    lf_row = jnp.minimum(f_row, 0.0) - jnp.log(1.0 + jnp.exp(-jnp.abs(f_row)))

    tt = lax.broadcasted_iota(jnp.int32, (chunk, chunk), 0)
    ss = lax.broadcasted_iota(jnp.int32, (chunk, chunk), 1)
    causal = ss <= tt
    diag = ss == tt
    lf_b = jnp.broadcast_to(lf_row, (chunk, chunk))
    cum_col = jnp.sum(jnp.where(causal, lf_b, 0.0), axis=1, keepdims=True)
    lf_col = jnp.sum(jnp.where(diag, lf_b, 0.0), axis=1, keepdims=True)
    i_col = jnp.sum(jnp.where(diag, jnp.broadcast_to(i_row, (chunk, chunk)), 0.0), axis=1, keepdims=True)
    cum_row = jnp.sum(jnp.where(tt <= ss, jnp.broadcast_to(lf_col, (chunk, chunk)), 0.0),
                      axis=0, keepdims=True)
    cum_last = jnp.sum(lf_row, axis=1, keepdims=True)
    m_prev = m_ref[...]

    log_w = jnp.where(causal, cum_col - cum_row + i_row, NEG_INF)
    log_inter = cum_col + m_prev
    m_row = jnp.maximum(log_inter, jnp.max(log_w, axis=1, keepdims=True))
    w_intra = jnp.exp(log_w - m_row)
    w_inter = jnp.exp(log_inter - m_row)

    q = q_ref[...]
    k = k_ref[...]
    v = v_ref[...].astype(BF16)
    c_state = c_ref[...]
    n_state = n_ref[...]
    attn = w_intra * lax.dot_general(q, k, NT_DIMS, preferred_element_type=F32)
    num = (w_inter * jnp.dot(q, c_state.astype(BF16), preferred_element_type=F32)
           + jnp.dot(attn.astype(BF16), v, preferred_element_type=F32))
    den = (w_inter * jnp.sum(q.astype(F32) * n_state, axis=1, keepdims=True)
           + jnp.sum(attn, axis=1, keepdims=True))
    h_out = num / jnp.maximum(jnp.abs(den), jnp.exp(-m_row))

    lte_col = cum_last - cum_col + i_col
    lte_row = cum_last - cum_row + i_row
    m_new = jnp.maximum(cum_last + m_prev, jnp.max(lte_row, axis=1, keepdims=True))
    w_end = jnp.exp(lte_col - m_new)
    decay = jnp.exp(cum_last + m_prev - m_new)
    kw = k.astype(F32) * w_end
    c_ref[...] = decay * c_state + jnp.dot(kw.T.astype(BF16), v, preferred_element_type=F32)
    n_ref[...] = decay * n_state + jnp.sum(kw, axis=0, keepdims=True)
    m_ref[...] = m_new

    mu = jnp.mean(h_out, axis=1, keepdims=True)
    hc = h_out - mu
    var = jnp.mean(hc * hc, axis=1, keepdims=True)
    h_norm = hc * lax.rsqrt(var + LN_EPS) * ng_ref[...]
    o_ref[...] = (_sigmoid(og_ref[...]) * h_norm).astype(o_ref.dtype)


def mlstm(qkv, qk_conv, gates, b_igate, b_fgate, norm_g, *, v_col0, o_col0):
    b, s, _ = qkv.shape
    n_heads = gates.shape[1]
    dk = qk_conv.shape[-1] // (2 * n_heads)
    dv = norm_g.shape[-1] // n_heads
    chunk = B_CHUNK
    return pl.pallas_call(
        functools.partial(_mlstm_kernel, chunk=chunk),
        grid=(b, n_heads, s // chunk),
        in_specs=[pl.BlockSpec(memory_space=pltpu.SMEM),
                  pl.BlockSpec(memory_space=pltpu.SMEM),
                  pl.BlockSpec((None, None, 2, chunk), lambda bb, h, c: (bb, h, 0, c)),
                  pl.BlockSpec((None, chunk, dk), lambda bb, h, c: (bb, c, h)),
                  pl.BlockSpec((None, chunk, dk), lambda bb, h, c: (bb, c, n_heads + h)),
                  pl.BlockSpec((None, chunk, dv), lambda bb, h, c: (bb, c, v_col0 // dv + h)),
                  pl.BlockSpec((None, chunk, dv), lambda bb, h, c: (bb, c, o_col0 // dv + h)),
                  pl.BlockSpec((1, dv), lambda bb, h, c: (0, h))],
        out_specs=pl.BlockSpec((None, chunk, dv), lambda bb, h, c: (bb, c, h)),
        out_shape=jax.ShapeDtypeStruct((b, s, n_heads * dv), BF16),
        scratch_shapes=[pltpu.VMEM((dk, dv), F32), pltpu.VMEM((1, dk), F32), pltpu.VMEM((1, 1), F32)],
        compiler_params=_cparams(("parallel", "parallel", "arbitrary")),
        name="mlstm",
    )(b_igate, b_fgate, gates, qk_conv, qk_conv, qkv, qkv, norm_g.reshape(1, -1))


MOBA_PENALTY = -3.0e38
MOBA_BIAS_PARTS = 3
MOBA_GROUP = 4
MOBA_KEY_CHUNK = 1024


def _moba_kernel(slopes_ref, q_ref, k_ref, v_ref, o_ref, kaug_ref, vaug_ref, kmean_ref, pen_ref,
                 *, blk, n_blk, top_k, scale):
    h = pl.program_id(1)
    slope = slopes_ref[h]
    dh = q_ref.shape[-1]
    lane = lax.broadcasted_iota(jnp.int32, (blk, LANES), 1)
    row = lax.broadcasted_iota(jnp.int32, (blk, 1), 0)

    kmean_ref[...] = jnp.zeros_like(kmean_ref)
    ones = jnp.ones((blk, dh), BF16)
    for n in range(n_blk):
        rows = slice(n * blk, (n + 1) * blk)
        kb = k_ref[rows, :]
        kmean_ref[n:n + 1, :] = jnp.mean(kb, axis=0, keepdims=True)
        aug = jnp.where(lane == n, 1.0, 0.0)
        rem = slope * (row + n * blk).astype(F32)
        for part in range(MOBA_BIAS_PARTS):
            piece = rem.astype(BF16).astype(F32)
            aug = aug + jnp.where(lane == n_blk + part, piece, 0.0)
            rem = rem - piece
        kaug_ref[rows, 0:dh] = kb.astype(BF16)
        kaug_ref[rows, dh:2 * dh] = aug.astype(BF16)
        vaug_ref[rows, 0:dh] = v_ref[rows, :].astype(BF16)
        vaug_ref[rows, dh:2 * dh] = ones

    ri = lax.broadcasted_iota(jnp.int32, (blk, blk), 0)
    cj = lax.broadcasted_iota(jnp.int32, (blk, blk), 1)

    blk_id = lax.broadcasted_iota(jnp.int32, (n_blk, blk), 0)
    tail_id = lax.broadcasted_iota(jnp.int32, (LANES - n_blk, blk), 0)
    tail = jnp.where(tail_id < MOBA_BIAS_PARTS, 1.0, 0.0)

    def select(qb, carry):
        q_rows = pl.ds(pl.multiple_of(qb * blk, blk), blk)
        gate = lax.dot_general(kmean_ref[0:n_blk, :], q_ref[q_rows, :] * scale, NT_DIMS,
                               precision=lax.Precision.HIGHEST, preferred_element_type=F32)
        gate = jnp.where(blk_id < qb, gate, NEG_INF)
        sel = jnp.zeros(gate.shape, F32)
        for j in range(top_k):
            mx = jnp.max(gate, axis=0, keepdims=True)
            idx = jnp.min(jnp.where(gate == mx, blk_id, n_blk), axis=0, keepdims=True)
            pick = blk_id == idx
            sel = sel + jnp.where(pick, jnp.where(qb > j, 1.0, 0.0), 0.0)
            gate = jnp.where(pick, NEG_INF, gate)
        pen_t = jnp.concatenate([jnp.where(sel > 0.0, 0.0, MOBA_PENALTY), tail], axis=0)
        pen_ref[qb] = pen_t.T.astype(BF16)
        return carry

    lax.fori_loop(0, n_blk, select, 0, unroll=MOBA_GROUP)

    def make_attend(n_keys):
        def attend(qb, carry):
            q_rows = pl.ds(pl.multiple_of(qb * blk, blk), blk)
            q16 = (q_ref[q_rows, :] * scale).astype(BF16)
            pen = pen_ref[qb]
            q_past = jnp.concatenate([q16, pen], axis=1)
            q_own = jnp.concatenate([q16, jnp.where(lane == qb, 0.0, pen.astype(F32)).astype(BF16)], axis=1)
            s_own = lax.dot_general(q_own, kaug_ref[q_rows, :], NT_DIMS, preferred_element_type=F32)
            s_own = jnp.where(ri >= cj, s_own, NEG_INF)
            m = jnp.max(s_own, axis=1, keepdims=True)
            acc = jnp.dot(jnp.exp(s_own - m).astype(BF16), vaug_ref[q_rows, :], preferred_element_type=F32)
            for k0 in range(0, n_keys, MOBA_KEY_CHUNK):
                keys = slice(k0, min(k0 + MOBA_KEY_CHUNK, n_keys))
                s = lax.dot_general(q_past, kaug_ref[keys, :], NT_DIMS, preferred_element_type=F32)
                m_new = jnp.maximum(m, jnp.max(s, axis=1, keepdims=True))
                acc = (jnp.exp(m - m_new) * acc
                       + jnp.dot(jnp.exp(s - m_new).astype(BF16), vaug_ref[keys, :], preferred_element_type=F32))
                m = m_new
            o_ref[q_rows, :] = (acc[:, :dh] / acc[:, dh:]).astype(o_ref.dtype)
            return carry
        return attend

    for first in range(0, n_blk, MOBA_GROUP):
        last = min(first + MOBA_GROUP, n_blk)
        lax.fori_loop(first, last, make_attend(last * blk), 0, unroll=MOBA_GROUP)


def moba_attention(qkv, n_heads, *, q_col, k_col, v_col):
    b, s, _ = qkv.shape
    dh = HEAD_DIM
    blk = MOBA_BLOCK
    n_blk = s // blk
    top_k = min(MOBA_TOPK, n_blk)
    assert n_blk + MOBA_BIAS_PARTS <= LANES and n_blk % SUBLANES == 0 and dh == LANES

    def head(col):
        return pl.BlockSpec((None, s, dh), lambda bb, h: (bb, 0, col + h))

    return pl.pallas_call(
        functools.partial(_moba_kernel, blk=blk, n_blk=n_blk, top_k=top_k, scale=dh ** -0.5),
        grid=(b, n_heads),
        in_specs=[pl.BlockSpec(memory_space=pltpu.SMEM), head(q_col), head(k_col), head(v_col)],
        out_specs=pl.BlockSpec((None, s, dh), lambda bb, h: (bb, 0, h)),
        out_shape=jax.ShapeDtypeStruct((b, s, n_heads * dh), BF16),
        scratch_shapes=[pltpu.VMEM((s, 2 * dh), BF16), pltpu.VMEM((s, 2 * dh), BF16),
                        pltpu.VMEM((LANES, dh), F32), pltpu.VMEM((n_blk, blk, LANES), BF16)],
        compiler_params=_cparams(("parallel", "parallel")),
        name="moba_attention",
    )(_alibi_slopes(n_heads), qkv, qkv, qkv)


CONV_HALO = 32
LN_ROWS = 16


def _conv31_kernel(a_ref, ah_ref, b_ref, bh_ref, w_ref, g_ref, beta_ref, o_ref, glu_ref, conv_ref, shift_ref,
                   *, t_rows, width, channels):
    i = pl.program_id(1)
    halo = ah_ref[...] * _sigmoid(bh_ref[...])
    glu_ref[0:CONV_HALO, :] = jnp.where(i > 0, halo, 0.0)
    glu_ref[CONV_HALO:CONV_HALO + t_rows, :] = a_ref[...] * _sigmoid(b_ref[...])
    off = CONV_HALO - (width - 1)

    strip_rows = CONV_HALO + t_rows - SUBLANES

    def cols(c, carry):
        cs = pl.ds(pl.multiple_of(c * LANES, LANES), LANES)
        for phase in range(1, SUBLANES):
            shift_ref[phase, 0:strip_rows, :] = glu_ref[phase:phase + strip_rows, cs]
        acc = jnp.zeros((t_rows, LANES), F32)
        for tap in range(width):
            phase, base = (off + tap) % SUBLANES, (off + tap) // SUBLANES * SUBLANES
            src = glu_ref[base:base + t_rows, cs] if phase == 0 else shift_ref[phase, base:base + t_rows, :]
            acc = acc + w_ref[tap:tap + 1, cs] * src
        conv_ref[:, cs] = acc
        return carry

    lax.fori_loop(0, channels // LANES, cols, 0)

    def rows(r, carry):
        rs = pl.ds(pl.multiple_of(r * LN_ROWS, LN_ROWS), LN_ROWS)
        y = _ln_rows(conv_ref[rs, :], g_ref[...], beta_ref[...])
        o_ref[rs, :] = (y * _sigmoid(y)).astype(o_ref.dtype)
        return carry

    lax.fori_loop(0, t_rows // LN_ROWS, rows, 0, unroll=2)


def glu_conv_ln_silu(qkv, conv_w, ln_g, ln_b, *, a_col0, b_col0, channels, t_rows):
    b, s, _ = qkv.shape
    width = conv_w.shape[0]
    hb = t_rows // CONV_HALO
    ca, cb = a_col0 // channels, b_col0 // channels

    def cur(col):
        return pl.BlockSpec((None, t_rows, channels), lambda bb, i: (bb, i, col))

    def halo(col):
        return pl.BlockSpec((None, CONV_HALO, channels), lambda bb, i: (bb, jnp.maximum(i * hb - 1, 0), col))

    vec = pl.BlockSpec((1, channels), lambda bb, i: (0, 0))
    return pl.pallas_call(
        functools.partial(_conv31_kernel, t_rows=t_rows, width=width, channels=channels),
        grid=(b, s // t_rows),
        in_specs=[cur(ca), halo(ca), cur(cb), halo(cb),
                  pl.BlockSpec((width, channels), lambda bb, i: (0, 0)), vec, vec],
        out_specs=pl.BlockSpec((None, t_rows, channels), lambda bb, i: (bb, i, 0)),
        out_shape=jax.ShapeDtypeStruct((b, s, channels), BF16),
        scratch_shapes=[pltpu.VMEM((CONV_HALO + t_rows, channels), F32), pltpu.VMEM((t_rows, channels), F32),
                        pltpu.VMEM((SUBLANES, CONV_HALO + t_rows - SUBLANES, LANES), F32)],
        compiler_params=_cparams(("parallel", "parallel")),
        name="glu_conv_ln_silu",
    )(qkv, qkv, qkv, qkv, conv_w, ln_g.reshape(1, -1), ln_b.reshape(1, -1))


DOWN_TILES = dict(tm=2048, tn=1024, tk=1024)
PANEL_TILES = dict(tm=2048, tn=256)
ROW_TILE = 256


def _ffn(u, w_up, w_down):
    hmid = matmul_swiglu(u, w_up, **PANEL_TILES)
    return matmul_f32(hmid, w_down, w_down.shape[1], **DOWN_TILES)


def _mixer_dilated_mlstm(u, bsz, seq, w_in, conv_qk, b_igate, b_fgate, norm_g, w_out):
    d = u.shape[1]
    a_width = d // 2
    a_heads = a_width // HEAD_DIM
    b_width = d - a_width
    qk_width = b_width // 2
    n_main = 3 * a_width + 2 * qk_width + 2 * b_width
    w_in_t = w_in.T
    proj = matmul_rows([u], w_in_t, n_main, w_is_transposed=True, **PANEL_TILES).reshape(bsz, seq, n_main)
    gates = matmul_few_cols(u, w_in_t, n_main, 2 * B_HEADS, tm=PANEL_TILES["tm"])
    gates = gates[:, :2 * B_HEADS].reshape(bsz, seq, 2, B_HEADS).transpose(0, 3, 2, 1)

    hb = a_width // HEAD_DIM
    a_out = dilated_attention(proj, a_heads, q_col=0, k_col=hb, v_col=2 * hb)
    qk_col0 = 3 * a_width
    qk_conv = conv4_silu(proj, conv_qk, col0=qk_col0, width_cols=2 * qk_width,
                         k_scale=(qk_width // B_HEADS) ** -0.5, t_rows=512, tc=512)
    b_out = mlstm(proj, qk_conv, gates, b_igate, b_fgate, norm_g,
                  v_col0=qk_col0 + 2 * qk_width, o_col0=qk_col0 + 2 * qk_width + b_width)
    halves = [t.reshape(bsz * seq, -1) for t in (a_out, b_out)]
    return matmul_rows(halves, w_out, d, **PANEL_TILES)


def _mixer_moba_conv(u, bsz, seq, w_in, conv_dw, conv_ln_g, conv_ln_b, w_out):
    d = u.shape[1]
    c_width = d // 2
    c_heads = c_width // HEAD_DIM
    d_ch = d - c_width
    n_main = 3 * c_width + 2 * d_ch
    proj = matmul_rows([u], w_in, n_main, **PANEL_TILES).reshape(bsz, seq, n_main)
    c_out = moba_attention(proj, c_heads, q_col=0, k_col=c_heads, v_col=2 * c_heads)
    d_out = glu_conv_ln_silu(proj, conv_dw, conv_ln_g, conv_ln_b, a_col0=3 * c_width,
                             b_col0=3 * c_width + d_ch, channels=d_ch, t_rows=ROW_TILE)
    halves = [t.reshape(bsz * seq, -1) for t in (c_out, d_out)]
    return matmul_rows(halves, w_out, d, **PANEL_TILES)


def kernel(x, c, l0_w_ada, l0_b_ada, l0_ln_g, l0_ln_b, l0_ffn1_w_up, l0_ffn1_w_down, l0_ffn2_w_up, l0_ffn2_w_down, l0_w_in, l0_conv_qk, l0_b_igate, l0_b_fgate, l0_mlstm_norm_g, l0_w_out, l1_w_ada, l1_b_ada, l1_ln_g, l1_ln_b, l1_ffn1_w_up, l1_ffn1_w_down, l1_ffn2_w_up, l1_ffn2_w_down, l1_w_in, l1_conv_dw, l1_conv_ln_g, l1_conv_ln_b, l1_w_out):
    bsz, seq, d = x.shape
    depth = 2
    alpha = (2 * depth) ** 0.25
    layers = (
        (l0_w_ada, l0_b_ada, l0_ln_g, l0_ln_b, l0_ffn1_w_up, l0_ffn1_w_down, l0_ffn2_w_up, l0_ffn2_w_down),
        (l1_w_ada, l1_b_ada, l1_ln_g, l1_ln_b, l1_ffn1_w_up, l1_ffn1_w_down, l1_ffn2_w_up, l1_ffn2_w_down),
    )
    mixers = (
        lambda u: _mixer_dilated_mlstm(u, bsz, seq, l0_w_in, l0_conv_qk, l0_b_igate, l0_b_fgate,
                                       l0_mlstm_norm_g, l0_w_out),
        lambda u: _mixer_moba_conv(u, bsz, seq, l1_w_in, l1_conv_dw, l1_conv_ln_g, l1_conv_ln_b, l1_w_out),
    )
    c_pad = jnp.pad(c, ((0, SUBLANES - bsz), (0, 0)))
    mods = [ada_modulation(c_pad, p[0], p[1], tn=512)[:bsz].reshape(bsz, 3 * N_SUBLAYERS, 1, d) for p in layers]

    x2 = x.reshape(bsz * seq, d)
    u = modulate(x2, mods[0], 0, 1, rows_per_batch=seq, tm=ROW_TILE)
    for layer in range(depth):
        _, _, ln_g, ln_b, f1_up, f1_down, f2_up, f2_down = layers[layer]
        branches = (
            (0.5, lambda v: _ffn(v, f1_up, f1_down)),
            (1.0, mixers[layer]),
            (0.5, lambda v: _ffn(v, f2_up, f2_down)),
        )
        for j, (res_w, fn) in enumerate(branches):
            y = fn(u)
            if j + 1 < N_SUBLAYERS:
                nxt, nj = mods[layer], (3 * (j + 1), 3 * (j + 1) + 1)
            elif layer + 1 < depth:
                nxt, nj = mods[layer + 1], (0, 1)
            else:
                nxt, nj = None, None
            x2, u = ln_residual(x2, y, mods[layer], 3 * j + 2, ln_g, ln_b, j, res_w, alpha, nxt, nj,
                                rows_per_batch=seq, tm=ROW_TILE)
    return x2.reshape(bsz, seq, d)
```

```python
import functools

import jax
import jax.numpy as jnp
import numpy as np
from jax import lax
from jax.experimental import pallas as pl
from jax.experimental.pallas import tpu as pltpu

F32 = jnp.float32
BF16 = jnp.bfloat16
NEG_INF = float("-inf")

LANES = 128
SUBLANES = 8
VMEM_LIMIT_BYTES = 56 * 1024 * 1024
SWIGLU_VMEM_LIMIT_BYTES = 60 * 1024 * 1024

HEAD_DIM = 128
A_PATTERNS = ((128, 1), (512, 4), (2048, 16))
A_STEPS = 128
A_SPAN = 2048
B_HEADS = 4
B_CONV_WIDTH = 4
B_CHUNK = 256
MOBA_BLOCK = 256
MOBA_TOPK = 3
D_CONV_WIDTH = 31
N_SUBLAYERS = 3
LN_EPS = 1e-5

NT_DIMS = (((1,), (1,)), ((), ()))


def _cparams(semantics, vmem_limit_bytes=VMEM_LIMIT_BYTES):
    return pltpu.CompilerParams(dimension_semantics=semantics, vmem_limit_bytes=vmem_limit_bytes)


def _sigmoid(x):
    return 1.0 / (1.0 + jnp.exp(-x))


def _alibi_slopes(n_heads):
    return jnp.asarray(2.0 ** (-8.0 * np.arange(1, n_heads + 1) / n_heads), F32)


def _mm_f32_kernel(a_ref, w_ref, o_ref):
    @pl.when(pl.program_id(2) == 0)
    def _init():
        o_ref[...] = jnp.zeros_like(o_ref)

    o_ref[...] += jnp.dot(a_ref[...], w_ref[...].astype(BF16), preferred_element_type=F32)


def matmul_f32(a, w, n_out, *, tm, tn, tk):
    m, k = a.shape
    grid = (m // tm, n_out // tn, k // tk)
    return pl.pallas_call(
        _mm_f32_kernel,
        grid=grid,
        in_specs=[pl.BlockSpec((tm, tk), lambda i, j, kk: (i, kk)),
                  pl.BlockSpec((tk, tn), lambda i, j, kk: (kk, j))],
        out_specs=pl.BlockSpec((tm, tn), lambda i, j, kk: (i, j)),
        out_shape=jax.ShapeDtypeStruct((m, n_out), F32),
        compiler_params=_cparams(("parallel", "parallel", "arbitrary")),
        name="matmul_f32",
    )(a, w)


def _mm_rows_kernel(*refs, n_a, w_is_transposed):
    a_refs, w_ref, o_ref = refs[:n_a], refs[n_a], refs[n_a + 1]
    acc, off = None, 0
    for a_ref in a_refs:
        ka = a_ref.shape[1]
        if w_is_transposed:
            part = lax.dot_general(a_ref[...], w_ref[:, off:off + ka].astype(BF16), NT_DIMS,
                                   preferred_element_type=F32)
        else:
            part = jnp.dot(a_ref[...], w_ref[off:off + ka, :].astype(BF16), preferred_element_type=F32)
        acc = part if acc is None else acc + part
        off += ka
    o_ref[...] = acc.astype(o_ref.dtype)


def matmul_rows(a_list, w, n_out, *, tm, tn, col_block0=0, single_buffer_a=False, w_is_transposed=False):
    m = a_list[0].shape[0]
    k = w.shape[1] if w_is_transposed else w.shape[0]
    assert sum(a.shape[1] for a in a_list) == k
    mode = dict(pipeline_mode=pl.Buffered(1)) if single_buffer_a else {}
    in_specs = [pl.BlockSpec((tm, a.shape[1]), lambda i, j: (i, 0), **mode) for a in a_list]
    if w_is_transposed:
        in_specs.append(pl.BlockSpec((tn, k), lambda i, j: (j + col_block0, 0)))
    else:
        in_specs.append(pl.BlockSpec((k, tn), lambda i, j: (0, j + col_block0)))
    return pl.pallas_call(
        functools.partial(_mm_rows_kernel, n_a=len(a_list), w_is_transposed=w_is_transposed),
        grid=(m // tm, pl.cdiv(n_out, tn)),
        in_specs=in_specs,
        out_specs=pl.BlockSpec((tm, tn), lambda i, j: (i, j)),
        out_shape=jax.ShapeDtypeStruct((m, n_out), F32),
        compiler_params=_cparams(("parallel", "arbitrary")),
        name="matmul_rows",
    )(*a_list, w)


def _mm_few_cols_kernel(a_ref, w_ref, o_ref):
    n_valid, k = w_ref.shape
    w = jnp.concatenate([w_ref[...], jnp.zeros((o_ref.shape[1] - n_valid, k), F32)], axis=0)
    o_ref[...] = lax.dot_general(a_ref[...], w.astype(BF16), NT_DIMS, preferred_element_type=F32)


def matmul_few_cols(a, w_t, row0, n_cols, *, tm):
    m, k = a.shape
    assert n_cols % SUBLANES == 0 and row0 % n_cols == 0
    return pl.pallas_call(
        _mm_few_cols_kernel,
        grid=(m // tm,),
        in_specs=[pl.BlockSpec((tm, k), lambda i: (i, 0)),
                  pl.BlockSpec((n_cols, k), lambda i: (row0 // n_cols, 0))],
        out_specs=pl.BlockSpec((tm, LANES), lambda i: (i, 0)),
        out_shape=jax.ShapeDtypeStruct((m, LANES), F32),
        compiler_params=_cparams(("parallel",)),
        name="matmul_few_cols",
    )(a, w_t)


def _mm_swiglu_kernel(a_ref, wg_ref, wv_ref, o_ref):
    a = a_ref[...]
    g = jnp.dot(a, wg_ref[...].astype(BF16), preferred_element_type=F32)
    v = jnp.dot(a, wv_ref[...].astype(BF16), preferred_element_type=F32)
    o_ref[...] = (g * _sigmoid(g) * v).astype(o_ref.dtype)


def matmul_swiglu(a, w_up, *, tm, tn):
    m, k = a.shape
    f = w_up.shape[1] // 2
    nh = f // tn
    return pl.pallas_call(
        _mm_swiglu_kernel,
        grid=(m // tm, nh),
        in_specs=[pl.BlockSpec((tm, k), lambda i, j: (i, 0)),
                  pl.BlockSpec((k, tn), lambda i, j: (0, j)),
                  pl.BlockSpec((k, tn), lambda i, j: (0, j + nh))],
        out_specs=pl.BlockSpec((tm, tn), lambda i, j: (i, j)),
        out_shape=jax.ShapeDtypeStruct((m, f), BF16),
        compiler_params=_cparams(("parallel", "arbitrary"), SWIGLU_VMEM_LIMIT_BYTES),
        name="matmul_swiglu",
    )(a, w_up, w_up)


def _ada_kernel(c_ref, w_ref, b_ref, o_ref):
    c = c_ref[...]
    a = (c * _sigmoid(c)).astype(BF16)
    o_ref[...] = jnp.dot(a, w_ref[...].astype(BF16), preferred_element_type=F32) + b_ref[...]


def ada_modulation(c_pad, w_ada, b_ada, *, tn):
    rows, d = c_pad.shape
    n = w_ada.shape[1]
    return pl.pallas_call(
        _ada_kernel,
        grid=(n // tn,),
        in_specs=[pl.BlockSpec((rows, d), lambda j: (0, 0)),
                  pl.BlockSpec((d, tn), lambda j: (0, j)),
                  pl.BlockSpec((1, tn), lambda j: (0, j))],
        out_specs=pl.BlockSpec((rows, tn), lambda j: (0, j)),
        out_shape=jax.ShapeDtypeStruct((rows, n), F32),
        compiler_params=_cparams(("parallel",)),
        name="ada_modulation",
    )(c_pad, w_ada, b_ada.reshape(1, n))


def _modulate_kernel(x_ref, sc_ref, sh_ref, u_ref):
    u_ref[...] = (x_ref[...] * (1.0 + sc_ref[...]) + sh_ref[...]).astype(u_ref.dtype)


def modulate(x2, mod, j_shift, j_scale, *, rows_per_batch, tm):
    m, d = x2.shape
    bpb = rows_per_batch // tm

    def mspec(j):
        return pl.BlockSpec((None, None, 1, d), lambda i: (i // bpb, j, 0, 0))

    return pl.pallas_call(
        _modulate_kernel,
        grid=(m // tm,),
        in_specs=[pl.BlockSpec((tm, d), lambda i: (i, 0)), mspec(j_scale), mspec(j_shift)],
        out_specs=pl.BlockSpec((tm, d), lambda i: (i, 0)),
        out_shape=jax.ShapeDtypeStruct((m, d), BF16),
        compiler_params=_cparams(("parallel",)),
        name="modulate",
    )(x2, mod, mod)


def _ln_rows(z, g, b):
    mu = jnp.mean(z, axis=-1, keepdims=True)
    zc = z - mu
    var = jnp.mean(zc * zc, axis=-1, keepdims=True)
    return zc * lax.rsqrt(var + LN_EPS) * g + b


def _ln_res_kernel(x_ref, y_ref, gate_ref, g_ref, b_ref, *rest, alpha, res_w, tm, with_u):
    tile = (SUBLANES, x_ref.shape[-1])
    if with_u:
        sc_ref, sh_ref, xo_ref, uo_ref, gw_ref, g1_ref, b1_ref, g2_ref, b2_ref = rest
        g2_ref[...] = jnp.broadcast_to(g_ref[...] * (1.0 + sc_ref[...]), tile)
        b2_ref[...] = jnp.broadcast_to(b_ref[...] * (1.0 + sc_ref[...]) + sh_ref[...], tile)
    else:
        xo_ref, gw_ref, g1_ref, b1_ref = rest
    gw_ref[...] = jnp.broadcast_to(res_w * (1.0 + gate_ref[...]), tile)
    g1_ref[...] = jnp.broadcast_to(g_ref[...], tile)
    b1_ref[...] = jnp.broadcast_to(b_ref[...], tile)
    inv_d = 1.0 / x_ref.shape[-1]

    def rows(r, carry):
        sl = pl.ds(pl.multiple_of(r * SUBLANES, SUBLANES), SUBLANES)
        z = alpha * x_ref[sl, :] + gw_ref[...] * y_ref[sl, :]
        mu = jnp.sum(z, axis=-1, keepdims=True) * inv_d
        zc = z - mu
        var = jnp.sum(zc * zc, axis=-1, keepdims=True) * inv_d
        t = zc * lax.rsqrt(var + LN_EPS)
        xo_ref[sl, :] = t * g1_ref[...] + b1_ref[...]
        if with_u:
            uo_ref[sl, :] = (t * g2_ref[...] + b2_ref[...]).astype(uo_ref.dtype)
        return carry

    lax.fori_loop(0, tm // SUBLANES, rows, 0, unroll=4)


def ln_residual(x2, y2, mod, j_gate, ln_g, ln_b, j_ln, res_w, alpha, next_mod, next_j, *, rows_per_batch, tm):
    m, d = x2.shape
    bpb = rows_per_batch // tm
    with_u = next_mod is not None

    def mspec(j):
        return pl.BlockSpec((None, None, 1, d), lambda i: (i // bpb, j, 0, 0))

    row_spec = pl.BlockSpec((tm, d), lambda i: (i, 0))
    ln_spec = pl.BlockSpec((None, 1, d), lambda i: (j_ln, 0, 0))
    in_specs = [row_spec, row_spec, mspec(j_gate), ln_spec, ln_spec]
    args = [x2, y2, mod, ln_g.reshape(N_SUBLAYERS, 1, d), ln_b.reshape(N_SUBLAYERS, 1, d)]
    out_specs = [row_spec]
    out_shape = [jax.ShapeDtypeStruct((m, d), F32)]
    if with_u:
        j_shift, j_scale = next_j
        in_specs += [mspec(j_scale), mspec(j_shift)]
        args += [next_mod, next_mod]
        out_specs.append(row_spec)
        out_shape.append(jax.ShapeDtypeStruct((m, d), BF16))
    outs = pl.pallas_call(
        functools.partial(_ln_res_kernel, alpha=alpha, res_w=res_w, tm=tm, with_u=with_u),
        grid=(m // tm,),
        in_specs=in_specs,
        out_specs=out_specs,
        out_shape=out_shape,
        scratch_shapes=[pltpu.VMEM((SUBLANES, d), F32)] * (5 if with_u else 3),
        compiler_params=_cparams(("parallel",)),
        name="ln_residual",
    )(*args)
    return (outs[0], outs[1]) if with_u else (outs[0], None)


A_SPLIT = 4


def _dilated_kernel(slopes_ref, q_ref, kc_ref, kp_ref, vc_ref, vp_ref, o_ref,
                    k4_ref, v4_ref, bias_ref, *acc_refs, span, steps, patterns, scale):
    h = pl.program_id(1)
    i = pl.program_id(2)
    slope = slopes_ref[h]
    n_pat = len(patterns)
    o_acc = acc_refs[:n_pat]
    m_acc = acc_refs[n_pat:2 * n_pat]
    l_acc = acc_refs[2 * n_pat:]
    dh = q_ref.shape[-1]
    part = span // A_SPLIT

    for c in range(A_SPLIT):
        for dst, prev_ref, cur_ref in ((k4_ref, kp_ref, kc_ref), (v4_ref, vp_ref, vc_ref)):
            dst[c, 0:part, :] = prev_ref[pl.ds(c, part, stride=A_SPLIT), :]
            dst[c, part:2 * part, :] = cur_ref[pl.ds(c, part, stride=A_SPLIT), :]

    qi = lax.broadcasted_iota(jnp.int32, (steps, 2 * steps), 0)
    kj = lax.broadcasted_iota(jnp.int32, (steps, 2 * steps), 1)
    back = steps + qi - kj
    in_band = (back >= 0) & (back <= steps)
    backf = back.astype(F32)
    for g, (_, dil) in enumerate(patterns):
        bias = jnp.where(in_band, (-(slope * dil)) * backf, NEG_INF)
        bias_ref[2 * g] = bias
        bias_ref[2 * g + 1] = jnp.where(kj >= steps, bias, NEG_INF)
    ones = jnp.ones((2 * steps, dh), BF16)

    first_span = jnp.where(i > 0, 0, 1)

    def tile(g, dil, r, n):
        q_start = r + dil * steps * n
        q_rows = pl.ds(q_start, steps, stride=dil)
        q = q_ref[q_rows, :]
        if dil % A_SPLIT == 0:
            c, sub = r % A_SPLIT, dil // A_SPLIT
            k_rows = pl.ds((span + q_start - c - dil * steps) // A_SPLIT, 2 * steps, stride=sub)
            k, v = k4_ref[c, k_rows, :], v4_ref[c, k_rows, :]
        elif n == 0:
            k = jnp.concatenate([kp_ref[span - steps:span, :], kc_ref[0:steps, :]], axis=0)
            v = jnp.concatenate([vp_ref[span - steps:span, :], vc_ref[0:steps, :]], axis=0)
        else:
            k, v = kc_ref[q_start - steps:q_start + steps, :], vc_ref[q_start - steps:q_start + steps, :]
        v_ones = jnp.concatenate([v.astype(BF16), ones], axis=1)
        bias = bias_ref[2 * g] if n > 0 else bias_ref[2 * g + first_span]
        s = lax.dot_general((q * scale).astype(BF16), k.astype(BF16), NT_DIMS, preferred_element_type=F32) + bias
        mx = jnp.max(s, axis=1, keepdims=True)
        p = jnp.exp(s - mx)
        ov = jnp.dot(p.astype(BF16), v_ones, preferred_element_type=F32)
        o_acc[g][q_rows, :] = ov[:, :dh]
        m_acc[g][q_rows, :] = jnp.broadcast_to(mx, (steps, dh))
        l_acc[g][q_rows, :] = ov[:, dh:]

    for g, (_, dil) in enumerate(patterns):
        for t in range(span // steps):
            tile(g, dil, t % dil, t // dil)

    def merge(c, carry):
        sl = pl.ds(pl.multiple_of(c * steps, steps), steps)
        ms = [m[sl, :] for m in m_acc]
        top = functools.reduce(jnp.maximum, ms)
        num = jnp.zeros((steps, dh), F32)
        den = jnp.zeros((steps, dh), F32)
        for g in range(n_pat):
            w = jnp.exp(ms[g] - top)
            num = num + w * o_acc[g][sl, :]
            den = den + w * l_acc[g][sl, :]
        o_ref[sl, :] = (num / den).astype(o_ref.dtype)
        return carry

    lax.fori_loop(0, span // steps, merge, 0, unroll=2)


def dilated_attention(qkv, n_heads, *, q_col, k_col, v_col):
    b, s, _ = qkv.shape
    dh = HEAD_DIM
    span, steps = A_SPAN, A_STEPS
    n_pat = len(A_PATTERNS)

    def cur(col):
        return pl.BlockSpec((None, span, dh), lambda bb, h, i: (bb, i, col + h))

    def prev(col):
        return pl.BlockSpec((None, span, dh), lambda bb, h, i: (bb, jnp.maximum(i - 1, 0), col + h))

    assert all(dil == 1 or dil % A_SPLIT == 0 for _, dil in A_PATTERNS)
    part = span // A_SPLIT
    scratch = [pltpu.VMEM((A_SPLIT, 2 * part, dh), F32), pltpu.VMEM((A_SPLIT, 2 * part, dh), F32),
               pltpu.VMEM((2 * n_pat, steps, 2 * steps), F32)]
    scratch += [pltpu.VMEM((span, dh), F32) for _ in range(3 * n_pat)]
    return pl.pallas_call(
        functools.partial(_dilated_kernel, span=span, steps=steps, patterns=A_PATTERNS, scale=dh ** -0.5),
        grid=(b, n_heads, s // span),
        in_specs=[pl.BlockSpec(memory_space=pltpu.SMEM),
                  cur(q_col), cur(k_col), prev(k_col), cur(v_col), prev(v_col)],
        out_specs=pl.BlockSpec((None, span, dh), lambda bb, h, i: (bb, i, h)),
        out_shape=jax.ShapeDtypeStruct((b, s, n_heads * dh), BF16),
        scratch_shapes=scratch,
        compiler_params=_cparams(("parallel", "parallel", "arbitrary")),
        name="dilated_attention",
    )(_alibi_slopes(n_heads), qkv, qkv, qkv, qkv, qkv)


def _mlstm_kernel(big_ref, bfg_ref, gates_ref, qk_ref, cw_ref, v_ref, og_ref, ng_ref, o_ref,
                  buf_ref, c_ref, n_ref, m_ref, *, chunk, n_heads, dk, dv, k_scale):
    c = pl.program_id(1)
    width = cw_ref.shape[0]

    @pl.when(c == 0)
    def _init():
        c_ref[...] = jnp.zeros_like(c_ref)
        n_ref[...] = jnp.zeros_like(n_ref)
        m_ref[...] = jnp.zeros_like(m_ref)
        buf_ref[0:SUBLANES, :] = jnp.zeros((SUBLANES, buf_ref.shape[1]), F32)

    buf_ref[SUBLANES:SUBLANES + chunk, :] = qk_ref[...]
    off = SUBLANES - (width - 1)

    def conv_swish(col0):
        cols = slice(col0, col0 + dk)
        acc = jnp.zeros((chunk, dk), F32)
        for tap in range(width):
            acc = acc + cw_ref[tap:tap + 1, cols] * buf_ref[off + tap:off + tap + chunk, cols]
        return acc * _sigmoid(acc)

    tt = lax.broadcasted_iota(jnp.int32, (chunk, chunk), 0)
    ss = lax.broadcasted_iota(jnp.int32, (chunk, chunk), 1)
    causal = ss <= tt
    diag = ss == tt
    for h in range(n_heads):
        _mlstm_head(h, big_ref, bfg_ref, gates_ref, v_ref, og_ref, ng_ref, o_ref, c_ref, n_ref, m_ref,
                    q=conv_swish(h * dk).astype(BF16),
                    k=(conv_swish((n_heads + h) * dk) * k_scale).astype(BF16),
                    tt=tt, ss=ss, causal=causal, diag=diag, chunk=chunk, dv=dv)
    buf_ref[0:SUBLANES, :] = buf_ref[chunk:chunk + SUBLANES, :]


def _mlstm_head(h, big_ref, bfg_ref, gates_ref, v_ref, og_ref, ng_ref, o_ref, c_ref, n_ref, m_ref,
                *, q, k, tt, ss, causal, diag, chunk, dv):
    vcols = slice(h * dv, (h + 1) * dv)
    gates = gates_ref[h]
    i_row = gates[0:1, :] + big_ref[h]
    f_row = gates[1:2, :] + bfg_ref[h]
    lf_row = jnp.minimum(f_row, 0.0) - jnp.log(1.0 + jnp.exp(-jnp.abs(f_row)))

    lf_b = jnp.broadcast_to(lf_row, (chunk, chunk))
    cum_col = jnp.sum(jnp.where(causal, lf_b, 0.0), axis=1, keepdims=True)
    lf_col = jnp.sum(jnp.where(diag, lf_b, 0.0), axis=1, keepdims=True)
    i_col = jnp.sum(jnp.where(diag, jnp.broadcast_to(i_row, (chunk, chunk)), 0.0), axis=1, keepdims=True)
    cum_row = jnp.sum(jnp.where(tt <= ss, jnp.broadcast_to(lf_col, (chunk, chunk)), 0.0),
                      axis=0, keepdims=True)
    cum_last = jnp.sum(lf_row, axis=1, keepdims=True)
    m_prev = m_ref[h]

    log_w = jnp.where(causal, cum_col - cum_row + i_row, NEG_INF)
    log_inter = cum_col + m_prev
    m_row = jnp.maximum(log_inter, jnp.max(log_w, axis=1, keepdims=True))
    w_intra = jnp.exp(log_w - m_row)
    w_inter = jnp.exp(log_inter - m_row)

    v = v_ref[:, vcols].astype(BF16)
    c_state = c_ref[h]
    n_state = n_ref[h]
    attn = w_intra * lax.dot_general(q, k, NT_DIMS, preferred_element_type=F32)
    num = (w_inter * jnp.dot(q, c_state.astype(BF16), preferred_element_type=F32)
           + jnp.dot(attn.astype(BF16), v, preferred_element_type=F32))
    den = (w_inter * jnp.sum(q.astype(F32) * n_state, axis=1, keepdims=True)
           + jnp.sum(attn, axis=1, keepdims=True))
    h_out = num / jnp.maximum(jnp.abs(den), jnp.exp(-m_row))

    lte_col = cum_last - cum_col + i_col
    lte_row = cum_last - cum_row + i_row
    m_new = jnp.maximum(cum_last + m_prev, jnp.max(lte_row, axis=1, keepdims=True))
    w_end = jnp.exp(lte_col - m_new)
    decay = jnp.exp(cum_last + m_prev - m_new)
    kw = k.astype(F32) * w_end
    c_ref[h] = decay * c_state + jnp.dot(kw.T.astype(BF16), v, preferred_element_type=F32)
    n_ref[h] = decay * n_state + jnp.sum(kw, axis=0, keepdims=True)
    m_ref[h] = m_new

    mu = jnp.mean(h_out, axis=1, keepdims=True)
    hc = h_out - mu
    var = jnp.mean(hc * hc, axis=1, keepdims=True)
    h_norm = hc * lax.rsqrt(var + LN_EPS) * ng_ref[:, vcols]
    o_ref[:, vcols] = (_sigmoid(og_ref[:, vcols]) * h_norm).astype(o_ref.dtype)


def mlstm(qkv, conv_w, gates, b_igate, b_fgate, norm_g, *, qk_col0, v_col0, o_col0):
    b, s, _ = qkv.shape
    n_heads = gates.shape[1]
    qk_w = conv_w.shape[1]
    v_w = norm_g.shape[-1]
    dk, dv = qk_w // (2 * n_heads), v_w // n_heads
    chunk = B_CHUNK
    assert qk_col0 % qk_w == 0 and v_col0 % v_w == 0 and o_col0 % v_w == 0 and conv_w.shape[0] <= SUBLANES
    return pl.pallas_call(
        functools.partial(_mlstm_kernel, chunk=chunk, n_heads=n_heads, dk=dk, dv=dv, k_scale=dk ** -0.5),
        grid=(b, s // chunk),
        in_specs=[pl.BlockSpec(memory_space=pltpu.SMEM),
                  pl.BlockSpec(memory_space=pltpu.SMEM),
                  pl.BlockSpec((None, n_heads, 2, chunk), lambda bb, c: (bb, 0, 0, c)),
                  pl.BlockSpec((None, chunk, qk_w), lambda bb, c: (bb, c, qk_col0 // qk_w)),
                  pl.BlockSpec(conv_w.shape, lambda bb, c: (0, 0)),
                  pl.BlockSpec((None, chunk, v_w), lambda bb, c: (bb, c, v_col0 // v_w)),
                  pl.BlockSpec((None, chunk, v_w), lambda bb, c: (bb, c, o_col0 // v_w)),
                  pl.BlockSpec((1, v_w), lambda bb, c: (0, 0))],
        out_specs=pl.BlockSpec((None, chunk, v_w), lambda bb, c: (bb, c, 0)),
        out_shape=jax.ShapeDtypeStruct((b, s, v_w), BF16),
        scratch_shapes=[pltpu.VMEM((SUBLANES + chunk, qk_w), F32), pltpu.VMEM((n_heads, dk, dv), F32),
                        pltpu.VMEM((n_heads, 1, dk), F32), pltpu.VMEM((n_heads, 1, 1), F32)],
        compiler_params=_cparams(("parallel", "arbitrary")),
        name="mlstm",
    )(b_igate, b_fgate, gates, qkv, conv_w, qkv, qkv, norm_g.reshape(1, -1))


MOBA_PENALTY = -3.0e38
MOBA_BIAS_PARTS = 3
MOBA_GROUP = 4
MOBA_KEY_CHUNK = 1024


def _moba_kernel(slopes_ref, q_ref, k_ref, v_ref, o_ref, kaug_ref, vaug_ref, kmean_ref, pen_ref,
                 *, blk, n_blk, top_k, scale):
    h = pl.program_id(1)
    slope = slopes_ref[h]
    dh = q_ref.shape[-1]
    lane = lax.broadcasted_iota(jnp.int32, (blk, LANES), 1)
    row = lax.broadcasted_iota(jnp.int32, (blk, 1), 0)

    kmean_ref[...] = jnp.zeros_like(kmean_ref)
    ones = jnp.ones((blk, dh), BF16)
    for n in range(n_blk):
        rows = slice(n * blk, (n + 1) * blk)
        kb = k_ref[rows, :]
        kmean_ref[n:n + 1, :] = jnp.mean(kb, axis=0, keepdims=True)
        aug = jnp.where(lane == n, 1.0, 0.0)
        rem = slope * (row + n * blk).astype(F32)
        for part in range(MOBA_BIAS_PARTS):
            piece = rem.astype(BF16).astype(F32)
            aug = aug + jnp.where(lane == n_blk + part, piece, 0.0)
            rem = rem - piece
        kaug_ref[rows, 0:dh] = kb.astype(BF16)
        kaug_ref[rows, dh:2 * dh] = aug.astype(BF16)
        vaug_ref[rows, 0:dh] = v_ref[rows, :].astype(BF16)
        vaug_ref[rows, dh:2 * dh] = ones

    ri = lax.broadcasted_iota(jnp.int32, (blk, blk), 0)
    cj = lax.broadcasted_iota(jnp.int32, (blk, blk), 1)

    blk_id = lax.broadcasted_iota(jnp.int32, (n_blk, blk), 0)
    tail_id = lax.broadcasted_iota(jnp.int32, (LANES - n_blk, blk), 0)
    tail = jnp.where(tail_id < MOBA_BIAS_PARTS, 1.0, 0.0)

    def select(qb, carry):
        q_rows = pl.ds(pl.multiple_of(qb * blk, blk), blk)
        gate = lax.dot_general(kmean_ref[0:n_blk, :], q_ref[q_rows, :] * scale, NT_DIMS,
                               precision=lax.Precision.HIGHEST, preferred_element_type=F32)
        gate = jnp.where(blk_id < qb, gate, NEG_INF)
        sel = jnp.zeros(gate.shape, F32)
        for j in range(top_k):
            mx = jnp.max(gate, axis=0, keepdims=True)
            idx = jnp.min(jnp.where(gate == mx, blk_id, n_blk), axis=0, keepdims=True)
            pick = blk_id == idx
            sel = sel + jnp.where(pick, jnp.where(qb > j, 1.0, 0.0), 0.0)
            gate = jnp.where(pick, NEG_INF, gate)
        pen_t = jnp.concatenate([jnp.where(sel > 0.0, 0.0, MOBA_PENALTY), tail], axis=0)
        pen_ref[qb] = pen_t.T.astype(BF16)
        return carry

    lax.fori_loop(0, n_blk, select, 0, unroll=MOBA_GROUP)

    def make_attend(n_keys):
        def attend(qb, carry):
            q_rows = pl.ds(pl.multiple_of(qb * blk, blk), blk)
            q16 = (q_ref[q_rows, :] * scale).astype(BF16)
            pen = pen_ref[qb]
            q_past = jnp.concatenate([q16, pen], axis=1)
            q_own = jnp.concatenate([q16, jnp.where(lane == qb, 0.0, pen.astype(F32)).astype(BF16)], axis=1)
            s_own = lax.dot_general(q_own, kaug_ref[q_rows, :], NT_DIMS, preferred_element_type=F32)
            s_own = jnp.where(ri >= cj, s_own, NEG_INF)
            m = jnp.max(s_own, axis=1, keepdims=True)
            acc = jnp.dot(jnp.exp(s_own - m).astype(BF16), vaug_ref[q_rows, :], preferred_element_type=F32)
            for k0 in range(0, n_keys, MOBA_KEY_CHUNK):
                keys = slice(k0, min(k0 + MOBA_KEY_CHUNK, n_keys))
                s = lax.dot_general(q_past, kaug_ref[keys, :], NT_DIMS, preferred_element_type=F32)
                m_new = jnp.maximum(m, jnp.max(s, axis=1, keepdims=True))
                acc = (jnp.exp(m - m_new) * acc
                       + jnp.dot(jnp.exp(s - m_new).astype(BF16), vaug_ref[keys, :], preferred_element_type=F32))
                m = m_new
            o_ref[q_rows, :] = (acc[:, :dh] / acc[:, dh:]).astype(o_ref.dtype)
            return carry
        return attend

    for first in range(0, n_blk, MOBA_GROUP):
        last = min(first + MOBA_GROUP, n_blk)
        lax.fori_loop(first, last, make_attend(last * blk), 0, unroll=MOBA_GROUP)


def moba_attention(qkv, n_heads, *, q_col, k_col, v_col):
    b, s, _ = qkv.shape
    dh = HEAD_DIM
    blk = MOBA_BLOCK
    n_blk = s // blk
    top_k = min(MOBA_TOPK, n_blk)
    assert n_blk + MOBA_BIAS_PARTS <= LANES and n_blk % SUBLANES == 0 and dh == LANES

    def head(col):
        return pl.BlockSpec((None, s, dh), lambda bb, h: (bb, 0, col + h))

    return pl.pallas_call(
        functools.partial(_moba_kernel, blk=blk, n_blk=n_blk, top_k=top_k, scale=dh ** -0.5),
        grid=(b, n_heads),
        in_specs=[pl.BlockSpec(memory_space=pltpu.SMEM), head(q_col), head(k_col), head(v_col)],
        out_specs=pl.BlockSpec((None, s, dh), lambda bb, h: (bb, 0, h)),
        out_shape=jax.ShapeDtypeStruct((b, s, n_heads * dh), BF16),
        scratch_shapes=[pltpu.VMEM((s, 2 * dh), BF16), pltpu.VMEM((s, 2 * dh), BF16),
                        pltpu.VMEM((LANES, dh), F32), pltpu.VMEM((n_blk, blk, LANES), BF16)],
        compiler_params=_cparams(("parallel", "parallel")),
        name="moba_attention",
    )(_alibi_slopes(n_heads), qkv, qkv, qkv)


CONV_HALO = 32
LN_ROWS = 16


def _conv31_kernel(a_ref, ah_ref, b_ref, bh_ref, w_ref, g_ref, beta_ref, o_ref, glu_ref, conv_ref, shift_ref,
                   *, t_rows, width, channels):
    i = pl.program_id(1)
    halo = ah_ref[...] * _sigmoid(bh_ref[...])
    glu_ref[0:CONV_HALO, :] = jnp.where(i > 0, halo, 0.0)
    glu_ref[CONV_HALO:CONV_HALO + t_rows, :] = a_ref[...] * _sigmoid(b_ref[...])
    off = CONV_HALO - (width - 1)

    strip_rows = CONV_HALO + t_rows - SUBLANES

    def cols(c, carry):
        cs = pl.ds(pl.multiple_of(c * LANES, LANES), LANES)
        for phase in range(1, SUBLANES):
            shift_ref[phase, 0:strip_rows, :] = glu_ref[phase:phase + strip_rows, cs]
        acc = jnp.zeros((t_rows, LANES), F32)
        for tap in range(width):
            phase, base = (off + tap) % SUBLANES, (off + tap) // SUBLANES * SUBLANES
            src = glu_ref[base:base + t_rows, cs] if phase == 0 else shift_ref[phase, base:base + t_rows, :]
            acc = acc + w_ref[tap:tap + 1, cs] * src
        conv_ref[:, cs] = acc
        return carry

    lax.fori_loop(0, channels // LANES, cols, 0)

    def rows(r, carry):
        rs = pl.ds(pl.multiple_of(r * LN_ROWS, LN_ROWS), LN_ROWS)
        y = _ln_rows(conv_ref[rs, :], g_ref[...], beta_ref[...])
        o_ref[rs, :] = (y * _sigmoid(y)).astype(o_ref.dtype)
        return carry

    lax.fori_loop(0, t_rows // LN_ROWS, rows, 0, unroll=2)


def glu_conv_ln_silu(qkv, conv_w, ln_g, ln_b, *, a_col0, b_col0, channels, t_rows):
    b, s, _ = qkv.shape
    width = conv_w.shape[0]
    hb = t_rows // CONV_HALO
    ca, cb = a_col0 // channels, b_col0 // channels

    def cur(col):
        return pl.BlockSpec((None, t_rows, channels), lambda bb, i: (bb, i, col))

    def halo(col):
        return pl.BlockSpec((None, CONV_HALO, channels), lambda bb, i: (bb, jnp.maximum(i * hb - 1, 0), col))

    vec = pl.BlockSpec((1, channels), lambda bb, i: (0, 0))
    return pl.pallas_call(
        functools.partial(_conv31_kernel, t_rows=t_rows, width=width, channels=channels),
        grid=(b, s // t_rows),
        in_specs=[cur(ca), halo(ca), cur(cb), halo(cb),
                  pl.BlockSpec((width, channels), lambda bb, i: (0, 0)), vec, vec],
        out_specs=pl.BlockSpec((None, t_rows, channels), lambda bb, i: (bb, i, 0)),
        out_shape=jax.ShapeDtypeStruct((b, s, channels), BF16),
        scratch_shapes=[pltpu.VMEM((CONV_HALO + t_rows, channels), F32), pltpu.VMEM((t_rows, channels), F32),
                        pltpu.VMEM((SUBLANES, CONV_HALO + t_rows - SUBLANES, LANES), F32)],
        compiler_params=_cparams(("parallel", "parallel")),
        name="glu_conv_ln_silu",
    )(qkv, qkv, qkv, qkv, conv_w, ln_g.reshape(1, -1), ln_b.reshape(1, -1))


DOWN_TILES = dict(tm=2048, tn=1024, tk=1024)
PANEL_TILES = dict(tm=2048, tn=256)
ROW_TILE = 256


def _ffn(u, w_up, w_down):
    hmid = matmul_swiglu(u, w_up, **PANEL_TILES)
    return matmul_f32(hmid, w_down, w_down.shape[1], **DOWN_TILES)


def _mixer_dilated_mlstm(u, bsz, seq, w_in, conv_qk, b_igate, b_fgate, norm_g, w_out):
    d = u.shape[1]
    a_width = d // 2
    a_heads = a_width // HEAD_DIM
    b_width = d - a_width
    qk_width = b_width // 2
    n_main = 3 * a_width + 2 * qk_width + 2 * b_width
    w_in_t = w_in.T
    proj = matmul_rows([u], w_in_t, n_main, w_is_transposed=True, **PANEL_TILES).reshape(bsz, seq, n_main)
    gates = matmul_few_cols(u, w_in_t, n_main, 2 * B_HEADS, tm=PANEL_TILES["tm"])
    gates = gates[:, :2 * B_HEADS].reshape(bsz, seq, 2, B_HEADS).transpose(0, 3, 2, 1)

    hb = a_width // HEAD_DIM
    a_out = dilated_attention(proj, a_heads, q_col=0, k_col=hb, v_col=2 * hb)
    qk_col0 = 3 * a_width
    b_out = mlstm(proj, conv_qk, gates, b_igate, b_fgate, norm_g, qk_col0=qk_col0,
                  v_col0=qk_col0 + 2 * qk_width, o_col0=qk_col0 + 2 * qk_width + b_width)
    halves = [t.reshape(bsz * seq, -1) for t in (a_out, b_out)]
    return matmul_rows(halves, w_out, d, **PANEL_TILES)


def _mixer_moba_conv(u, bsz, seq, w_in, conv_dw, conv_ln_g, conv_ln_b, w_out):
    d = u.shape[1]
    c_width = d // 2
    c_heads = c_width // HEAD_DIM
    d_ch = d - c_width
    n_main = 3 * c_width + 2 * d_ch
    proj = matmul_rows([u], w_in, n_main, **PANEL_TILES).reshape(bsz, seq, n_main)
    c_out = moba_attention(proj, c_heads, q_col=0, k_col=c_heads, v_col=2 * c_heads)
    d_out = glu_conv_ln_silu(proj, conv_dw, conv_ln_g, conv_ln_b, a_col0=3 * c_width,
                             b_col0=3 * c_width + d_ch, channels=d_ch, t_rows=ROW_TILE)
    halves = [t.reshape(bsz * seq, -1) for t in (c_out, d_out)]
    return matmul_rows(halves, w_out, d, **PANEL_TILES)


def kernel(x, c, l0_w_ada, l0_b_ada, l0_ln_g, l0_ln_b, l0_ffn1_w_up, l0_ffn1_w_down, l0_ffn2_w_up, l0_ffn2_w_down, l0_w_in, l0_conv_qk, l0_b_igate, l0_b_fgate, l0_mlstm_norm_g, l0_w_out, l1_w_ada, l1_b_ada, l1_ln_g, l1_ln_b, l1_ffn1_w_up, l1_ffn1_w_down, l1_ffn2_w_up, l1_ffn2_w_down, l1_w_in, l1_conv_dw, l1_conv_ln_g, l1_conv_ln_b, l1_w_out):
    bsz, seq, d = x.shape
    depth = 2
    alpha = (2 * depth) ** 0.25
    layers = (
        (l0_w_ada, l0_b_ada, l0_ln_g, l0_ln_b, l0_ffn1_w_up, l0_ffn1_w_down, l0_ffn2_w_up, l0_ffn2_w_down),
        (l1_w_ada, l1_b_ada, l1_ln_g, l1_ln_b, l1_ffn1_w_up, l1_ffn1_w_down, l1_ffn2_w_up, l1_ffn2_w_down),
    )
    mixers = (
        lambda u: _mixer_dilated_mlstm(u, bsz, seq, l0_w_in, l0_conv_qk, l0_b_igate, l0_b_fgate,
                                       l0_mlstm_norm_g, l0_w_out),
        lambda u: _mixer_moba_conv(u, bsz, seq, l1_w_in, l1_conv_dw, l1_conv_ln_g, l1_conv_ln_b, l1_w_out),
    )
    c_pad = jnp.pad(c, ((0, SUBLANES - bsz), (0, 0)))
    mods = [ada_modulation(c_pad, p[0], p[1], tn=512)[:bsz].reshape(bsz, 3 * N_SUBLAYERS, 1, d) for p in layers]

    x2 = x.reshape(bsz * seq, d)
    u = modulate(x2, mods[0], 0, 1, rows_per_batch=seq, tm=ROW_TILE)
    for layer in range(depth):
        _, _, ln_g, ln_b, f1_up, f1_down, f2_up, f2_down = layers[layer]
        branches = (
            (0.5, lambda v: _ffn(v, f1_up, f1_down)),
            (1.0, mixers[layer]),
            (0.5, lambda v: _ffn(v, f2_up, f2_down)),
        )
        for j, (res_w, fn) in enumerate(branches):
            y = fn(u)
            if j + 1 < N_SUBLAYERS:
                nxt, nj = mods[layer], (3 * (j + 1), 3 * (j + 1) + 1)
            elif layer + 1 < depth:
                nxt, nj = mods[layer + 1], (0, 1)
            else:
                nxt, nj = None, None
            x2, u = ln_residual(x2, y, mods[layer], 3 * j + 2, ln_g, ln_b, j, res_w, alpha, nxt, nj,
                                rows_per_batch=seq, tm=ROW_TILE)
    return x2.reshape(bsz, seq, d)
```

```python
import functools

import jax
import jax.numpy as jnp
import numpy as np
from jax import lax
from jax.experimental import pallas as pl
from jax.experimental.pallas import tpu as pltpu

F32 = jnp.float32
BF16 = jnp.bfloat16
NEG_INF = float("-inf")

LANES = 128
SUBLANES = 8
VMEM_LIMIT_BYTES = 56 * 1024 * 1024
SWIGLU_VMEM_LIMIT_BYTES = 60 * 1024 * 1024

HEAD_DIM = 128
A_PATTERNS = ((128, 1), (512, 4), (2048, 16))
A_STEPS = 128
A_SPAN = 2048
B_HEADS = 4
B_CONV_WIDTH = 4
B_CHUNK = 256
MOBA_BLOCK = 256
MOBA_TOPK = 3
D_CONV_WIDTH = 31
N_SUBLAYERS = 3
LN_EPS = 1e-5

NT_DIMS = (((1,), (1,)), ((), ()))


def _cparams(semantics, vmem_limit_bytes=VMEM_LIMIT_BYTES):
    return pltpu.CompilerParams(dimension_semantics=semantics, vmem_limit_bytes=vmem_limit_bytes)


def _sigmoid(x):
    return 1.0 / (1.0 + jnp.exp(-x))


def _alibi_slopes(n_heads):
    return jnp.asarray(2.0 ** (-8.0 * np.arange(1, n_heads + 1) / n_heads), F32)


def _mm_f32_kernel(a_ref, w_ref, o_ref):
    @pl.when(pl.program_id(2) == 0)
    def _init():
        o_ref[...] = jnp.zeros_like(o_ref)

    o_ref[...] += jnp.dot(a_ref[...], w_ref[...].astype(BF16), preferred_element_type=F32)


def matmul_f32(a, w, n_out, *, tm, tn, tk):
    m, k = a.shape
    grid = (m // tm, n_out // tn, k // tk)
    return pl.pallas_call(
        _mm_f32_kernel,
        grid=grid,
        in_specs=[pl.BlockSpec((tm, tk), lambda i, j, kk: (i, kk)),
                  pl.BlockSpec((tk, tn), lambda i, j, kk: (kk, j))],
        out_specs=pl.BlockSpec((tm, tn), lambda i, j, kk: (i, j)),
        out_shape=jax.ShapeDtypeStruct((m, n_out), F32),
        compiler_params=_cparams(("parallel", "parallel", "arbitrary")),
        name="matmul_f32",
    )(a, w)


def _mm_rows_kernel(*refs, n_a, w_is_transposed):
    a_refs, w_ref, o_ref = refs[:n_a], refs[n_a], refs[n_a + 1]
    acc, off = None, 0
    for a_ref in a_refs:
        ka = a_ref.shape[1]
        if w_is_transposed:
            part = lax.dot_general(a_ref[...], w_ref[:, off:off + ka].astype(BF16), NT_DIMS,
                                   preferred_element_type=F32)
        else:
            part = jnp.dot(a_ref[...], w_ref[off:off + ka, :].astype(BF16), preferred_element_type=F32)
        acc = part if acc is None else acc + part
        off += ka
    o_ref[...] = acc.astype(o_ref.dtype)


def matmul_rows(a_list, w, n_out, *, tm, tn, col_block0=0, single_buffer_a=False, w_is_transposed=False):
    m = a_list[0].shape[0]
    k = w.shape[1] if w_is_transposed else w.shape[0]
    assert sum(a.shape[1] for a in a_list) == k
    mode = dict(pipeline_mode=pl.Buffered(1)) if single_buffer_a else {}
    in_specs = [pl.BlockSpec((tm, a.shape[1]), lambda i, j: (i, 0), **mode) for a in a_list]
    if w_is_transposed:
        in_specs.append(pl.BlockSpec((tn, k), lambda i, j: (j + col_block0, 0)))
    else:
        in_specs.append(pl.BlockSpec((k, tn), lambda i, j: (0, j + col_block0)))
    return pl.pallas_call(
        functools.partial(_mm_rows_kernel, n_a=len(a_list), w_is_transposed=w_is_transposed),
        grid=(m // tm, pl.cdiv(n_out, tn)),
        in_specs=in_specs,
        out_specs=pl.BlockSpec((tm, tn), lambda i, j: (i, j)),
        out_shape=jax.ShapeDtypeStruct((m, n_out), F32),
        compiler_params=_cparams(("parallel", "arbitrary")),
        name="matmul_rows",
    )(*a_list, w)


def _mm_few_cols_kernel(a_ref, w_ref, o_ref):
    n_valid, k = w_ref.shape
    w = jnp.concatenate([w_ref[...], jnp.zeros((o_ref.shape[1] - n_valid, k), F32)], axis=0)
    o_ref[...] = lax.dot_general(a_ref[...], w.astype(BF16), NT_DIMS, preferred_element_type=F32)


def matmul_few_cols(a, w_t, row0, n_cols, *, tm):
    m, k = a.shape
    assert n_cols % SUBLANES == 0 and row0 % n_cols == 0
    return pl.pallas_call(
        _mm_few_cols_kernel,
        grid=(m // tm,),
        in_specs=[pl.BlockSpec((tm, k), lambda i: (i, 0)),
                  pl.BlockSpec((n_cols, k), lambda i: (row0 // n_cols, 0))],
        out_specs=pl.BlockSpec((tm, LANES), lambda i: (i, 0)),
        out_shape=jax.ShapeDtypeStruct((m, LANES), F32),
        compiler_params=_cparams(("parallel",)),
        name="matmul_few_cols",
    )(a, w_t)


def _mm_swiglu_kernel(a_ref, wg_ref, wv_ref, o_ref):
    a = a_ref[...]
    g = jnp.dot(a, wg_ref[...].astype(BF16), preferred_element_type=F32)
    v = jnp.dot(a, wv_ref[...].astype(BF16), preferred_element_type=F32)
    o_ref[...] = (g * _sigmoid(g) * v).astype(o_ref.dtype)


def matmul_swiglu(a, w_up, *, tm, tn):
    m, k = a.shape
    f = w_up.shape[1] // 2
    nh = f // tn
    return pl.pallas_call(
        _mm_swiglu_kernel,
        grid=(m // tm, nh),
        in_specs=[pl.BlockSpec((tm, k), lambda i, j: (i, 0)),
                  pl.BlockSpec((k, tn), lambda i, j: (0, j)),
                  pl.BlockSpec((k, tn), lambda i, j: (0, j + nh))],
        out_specs=pl.BlockSpec((tm, tn), lambda i, j: (i, j)),
        out_shape=jax.ShapeDtypeStruct((m, f), BF16),
        compiler_params=_cparams(("parallel", "arbitrary"), SWIGLU_VMEM_LIMIT_BYTES),
        name="matmul_swiglu",
    )(a, w_up, w_up)


def _ada_kernel(c_ref, w_ref, b_ref, o_ref):
    c = c_ref[...]
    a = (c * _sigmoid(c)).astype(BF16)
    o_ref[...] = jnp.dot(a, w_ref[...].astype(BF16), preferred_element_type=F32) + b_ref[...]


def ada_modulation(c_pad, w_ada, b_ada, *, tn):
    rows, d = c_pad.shape
    n = w_ada.shape[1]
    return pl.pallas_call(
        _ada_kernel,
        grid=(n // tn,),
        in_specs=[pl.BlockSpec((rows, d), lambda j: (0, 0)),
                  pl.BlockSpec((d, tn), lambda j: (0, j)),
                  pl.BlockSpec((1, tn), lambda j: (0, j))],
        out_specs=pl.BlockSpec((rows, tn), lambda j: (0, j)),
        out_shape=jax.ShapeDtypeStruct((rows, n), F32),
        compiler_params=_cparams(("parallel",)),
        name="ada_modulation",
    )(c_pad, w_ada, b_ada.reshape(1, n))


def _modulate_kernel(x_ref, sc_ref, sh_ref, u_ref):
    u_ref[...] = (x_ref[...] * (1.0 + sc_ref[...]) + sh_ref[...]).astype(u_ref.dtype)


def modulate(x2, mod, j_shift, j_scale, *, rows_per_batch, tm):
    m, d = x2.shape
    bpb = rows_per_batch // tm

    def mspec(j):
        return pl.BlockSpec((None, None, 1, d), lambda i: (i // bpb, j, 0, 0))

    return pl.pallas_call(
        _modulate_kernel,
        grid=(m // tm,),
        in_specs=[pl.BlockSpec((tm, d), lambda i: (i, 0)), mspec(j_scale), mspec(j_shift)],
        out_specs=pl.BlockSpec((tm, d), lambda i: (i, 0)),
        out_shape=jax.ShapeDtypeStruct((m, d), BF16),
        compiler_params=_cparams(("parallel",)),
        name="modulate",
    )(x2, mod, mod)


def _ln_rows(z, g, b):
    mu = jnp.mean(z, axis=-1, keepdims=True)
    zc = z - mu
    var = jnp.mean(zc * zc, axis=-1, keepdims=True)
    return zc * lax.rsqrt(var + LN_EPS) * g + b


def _ln_res_kernel(x_ref, y_ref, gate_ref, g_ref, b_ref, *rest, alpha, res_w, tm, with_u):
    tile = (SUBLANES, x_ref.shape[-1])
    if with_u:
        sc_ref, sh_ref, xo_ref, uo_ref, gw_ref, g1_ref, b1_ref, g2_ref, b2_ref = rest
        g2_ref[...] = jnp.broadcast_to(g_ref[...] * (1.0 + sc_ref[...]), tile)
        b2_ref[...] = jnp.broadcast_to(b_ref[...] * (1.0 + sc_ref[...]) + sh_ref[...], tile)
    else:
        xo_ref, gw_ref, g1_ref, b1_ref = rest
    gw_ref[...] = jnp.broadcast_to(res_w * (1.0 + gate_ref[...]), tile)
    g1_ref[...] = jnp.broadcast_to(g_ref[...], tile)
    b1_ref[...] = jnp.broadcast_to(b_ref[...], tile)
    inv_d = 1.0 / x_ref.shape[-1]

    def rows(r, carry):
        sl = pl.ds(pl.multiple_of(r * SUBLANES, SUBLANES), SUBLANES)
        z = alpha * x_ref[sl, :] + gw_ref[...] * y_ref[sl, :]
        mu = jnp.sum(z, axis=-1, keepdims=True) * inv_d
        zc = z - mu
        var = jnp.sum(zc * zc, axis=-1, keepdims=True) * inv_d
        t = zc * lax.rsqrt(var + LN_EPS)
        xo_ref[sl, :] = t * g1_ref[...] + b1_ref[...]
        if with_u:
            uo_ref[sl, :] = (t * g2_ref[...] + b2_ref[...]).astype(uo_ref.dtype)
        return carry

    lax.fori_loop(0, tm // SUBLANES, rows, 0, unroll=4)


def ln_residual(x2, y2, mod, j_gate, ln_g, ln_b, j_ln, res_w, alpha, next_mod, next_j, *, rows_per_batch, tm):
    m, d = x2.shape
    bpb = rows_per_batch // tm
    with_u = next_mod is not None

    def mspec(j):
        return pl.BlockSpec((None, None, 1, d), lambda i: (i // bpb, j, 0, 0))

    row_spec = pl.BlockSpec((tm, d), lambda i: (i, 0))
    ln_spec = pl.BlockSpec((None, 1, d), lambda i: (j_ln, 0, 0))
    in_specs = [row_spec, row_spec, mspec(j_gate), ln_spec, ln_spec]
    args = [x2, y2, mod, ln_g.reshape(N_SUBLAYERS, 1, d), ln_b.reshape(N_SUBLAYERS, 1, d)]
    out_specs = [row_spec]
    out_shape = [jax.ShapeDtypeStruct((m, d), F32)]
    if with_u:
        j_shift, j_scale = next_j
        in_specs += [mspec(j_scale), mspec(j_shift)]
        args += [next_mod, next_mod]
        out_specs.append(row_spec)
        out_shape.append(jax.ShapeDtypeStruct((m, d), BF16))
    outs = pl.pallas_call(
        functools.partial(_ln_res_kernel, alpha=alpha, res_w=res_w, tm=tm, with_u=with_u),
        grid=(m // tm,),
        in_specs=in_specs,
        out_specs=out_specs,
        out_shape=out_shape,
        scratch_shapes=[pltpu.VMEM((SUBLANES, d), F32)] * (5 if with_u else 3),
        compiler_params=_cparams(("parallel",)),
        name="ln_residual",
    )(*args)
    return (outs[0], outs[1]) if with_u else (outs[0], None)


A_SPLIT = 4


def _dilated_kernel(slopes_ref, q_ref, kc_ref, kp_ref, vc_ref, vp_ref, o_ref,
                    k4_ref, v4_ref, bias_ref, *acc_refs, span, steps, patterns, scale):
    h = pl.program_id(1)
    i = pl.program_id(2)
    slope = slopes_ref[h]
    n_pat = len(patterns)
    o_acc = acc_refs[:n_pat]
    m_acc = acc_refs[n_pat:2 * n_pat]
    l_acc = acc_refs[2 * n_pat:]
    dh = q_ref.shape[-1]
    part = span // A_SPLIT

    for c in range(A_SPLIT):
        for dst, prev_ref, cur_ref in ((k4_ref, kp_ref, kc_ref), (v4_ref, vp_ref, vc_ref)):
            dst[c, 0:part, :] = prev_ref[pl.ds(c, part, stride=A_SPLIT), :]
            dst[c, part:2 * part, :] = cur_ref[pl.ds(c, part, stride=A_SPLIT), :]

    qi = lax.broadcasted_iota(jnp.int32, (steps, 2 * steps), 0)
    kj = lax.broadcasted_iota(jnp.int32, (steps, 2 * steps), 1)
    back = steps + qi - kj
    in_band = (back >= 0) & (back <= steps)
    backf = back.astype(F32)
    for g, (_, dil) in enumerate(patterns):
        bias = jnp.where(in_band, (-(slope * dil)) * backf, NEG_INF)
        bias_ref[2 * g] = bias
        bias_ref[2 * g + 1] = jnp.where(kj >= steps, bias, NEG_INF)
    ones = jnp.ones((2 * steps, dh), BF16)

    first_span = jnp.where(i > 0, 0, 1)

    def tile(g, dil, r, n):
        q_start = r + dil * steps * n
        q_rows = pl.ds(q_start, steps, stride=dil)
        q = q_ref[q_rows, :]
        if dil % A_SPLIT == 0:
            c, sub = r % A_SPLIT, dil // A_SPLIT
            k_rows = pl.ds((span + q_start - c - dil * steps) // A_SPLIT, 2 * steps, stride=sub)
            k, v = k4_ref[c, k_rows, :], v4_ref[c, k_rows, :]
        elif n == 0:
            k = jnp.concatenate([kp_ref[span - steps:span, :], kc_ref[0:steps, :]], axis=0)
            v = jnp.concatenate([vp_ref[span - steps:span, :], vc_ref[0:steps, :]], axis=0)
        else:
            k, v = kc_ref[q_start - steps:q_start + steps, :], vc_ref[q_start - steps:q_start + steps, :]
        v_ones = jnp.concatenate([v.astype(BF16), ones], axis=1)
        bias = bias_ref[2 * g] if n > 0 else bias_ref[2 * g + first_span]
        s = lax.dot_general((q * scale).astype(BF16), k.astype(BF16), NT_DIMS, preferred_element_type=F32) + bias
        mx = jnp.max(s, axis=1, keepdims=True)
        p = jnp.exp(s - mx)
        ov = jnp.dot(p.astype(BF16), v_ones, preferred_element_type=F32)
        o_acc[g][q_rows, :] = ov[:, :dh]
        m_acc[g][q_rows, :] = jnp.broadcast_to(mx, (steps, dh))
        l_acc[g][q_rows, :] = ov[:, dh:]

    for g, (_, dil) in enumerate(patterns):
        for t in range(span // steps):
            tile(g, dil, t % dil, t // dil)

    def merge(c, carry):
        sl = pl.ds(pl.multiple_of(c * steps, steps), steps)
        ms = [m[sl, :] for m in m_acc]
        top = functools.reduce(jnp.maximum, ms)
        num = jnp.zeros((steps, dh), F32)
        den = jnp.zeros((steps, dh), F32)
        for g in range(n_pat):
            w = jnp.exp(ms[g] - top)
            num = num + w * o_acc[g][sl, :]
            den = den + w * l_acc[g][sl, :]
        o_ref[sl, :] = (num / den).astype(o_ref.dtype)
        return carry

    lax.fori_loop(0, span // steps, merge, 0, unroll=2)


def dilated_attention(qkv, n_heads, *, q_col, k_col, v_col):
    b, s, _ = qkv.shape
    dh = HEAD_DIM
    span, steps = A_SPAN, A_STEPS
    n_pat = len(A_PATTERNS)

    def cur(col):
        return pl.BlockSpec((None, span, dh), lambda bb, h, i: (bb, i, col + h))

    def prev(col):
        return pl.BlockSpec((None, span, dh), lambda bb, h, i: (bb, jnp.maximum(i - 1, 0), col + h))

    assert all(dil == 1 or dil % A_SPLIT == 0 for _, dil in A_PATTERNS)
    part = span // A_SPLIT
    scratch = [pltpu.VMEM((A_SPLIT, 2 * part, dh), F32), pltpu.VMEM((A_SPLIT, 2 * part, dh), F32),
               pltpu.VMEM((2 * n_pat, steps, 2 * steps), F32)]
    scratch += [pltpu.VMEM((span, dh), F32) for _ in range(3 * n_pat)]
    return pl.pallas_call(
        functools.partial(_dilated_kernel, span=span, steps=steps, patterns=A_PATTERNS, scale=dh ** -0.5),
        grid=(b, n_heads, s // span),
        in_specs=[pl.BlockSpec(memory_space=pltpu.SMEM),
                  cur(q_col), cur(k_col), prev(k_col), cur(v_col), prev(v_col)],
        out_specs=pl.BlockSpec((None, span, dh), lambda bb, h, i: (bb, i, h)),
        out_shape=jax.ShapeDtypeStruct((b, s, n_heads * dh), BF16),
        scratch_shapes=scratch,
        compiler_params=_cparams(("parallel", "parallel", "arbitrary")),
        name="dilated_attention",
    )(_alibi_slopes(n_heads), qkv, qkv, qkv, qkv, qkv)


def _mlstm_kernel(big_ref, bfg_ref, gates_ref, qk_ref, cw_ref, v_ref, og_ref, ng_ref, o_ref,
                  buf_ref, c_ref, n_ref, m_ref, *, chunk, n_heads, dk, dv, k_scale):
    c = pl.program_id(1)
    width = cw_ref.shape[0]

    @pl.when(c == 0)
    def _init():
        c_ref[...] = jnp.zeros_like(c_ref)
        n_ref[...] = jnp.zeros_like(n_ref)
        m_ref[...] = jnp.zeros_like(m_ref)
        buf_ref[0:SUBLANES, :] = jnp.zeros((SUBLANES, buf_ref.shape[1]), F32)

    buf_ref[SUBLANES:SUBLANES + chunk, :] = qk_ref[...]
    off = SUBLANES - (width - 1)

    def conv_swish(col0):
        cols = slice(col0, col0 + dk)
        acc = jnp.zeros((chunk, dk), F32)
        for tap in range(width):
            acc = acc + cw_ref[tap:tap + 1, cols] * buf_ref[off + tap:off + tap + chunk, cols]
        return acc * _sigmoid(acc)

    tt = lax.broadcasted_iota(jnp.int32, (chunk, chunk), 0)
    ss = lax.broadcasted_iota(jnp.int32, (chunk, chunk), 1)
    causal = ss <= tt
    diag = ss == tt
    for h in range(n_heads):
        _mlstm_head(h, big_ref, bfg_ref, gates_ref, v_ref, og_ref, ng_ref, o_ref, c_ref, n_ref, m_ref,
                    q=conv_swish(h * dk).astype(BF16),
                    k=(conv_swish((n_heads + h) * dk) * k_scale).astype(BF16),
                    tt=tt, ss=ss, causal=causal, diag=diag, chunk=chunk, dv=dv)
    buf_ref[0:SUBLANES, :] = buf_ref[chunk:chunk + SUBLANES, :]


def _mlstm_head(h, big_ref, bfg_ref, gates_ref, v_ref, og_ref, ng_ref, o_ref, c_ref, n_ref, m_ref,
                *, q, k, tt, ss, causal, diag, chunk, dv):
    vcols = slice(h * dv, (h + 1) * dv)
    gates = gates_ref[h]
    i_row = gates[0:1, :] + big_ref[h]
    f_row = gates[1:2, :] + bfg_ref[h]
    lf_row = jnp.minimum(f_row, 0.0) - jnp.log(1.0 + jnp.exp(-jnp.abs(f_row)))

    lf_b = jnp.broadcast_to(lf_row, (chunk, chunk))
    cum_col = jnp.sum(jnp.where(causal, lf_b, 0.0), axis=1, keepdims=True)
    lf_col = jnp.sum(jnp.where(diag, lf_b, 0.0), axis=1, keepdims=True)
    i_col = jnp.sum(jnp.where(diag, jnp.broadcast_to(i_row, (chunk, chunk)), 0.0), axis=1, keepdims=True)
    cum_row = jnp.sum(jnp.where(tt <= ss, jnp.broadcast_to(lf_col, (chunk, chunk)), 0.0),
                      axis=0, keepdims=True)
    cum_last = jnp.sum(lf_row, axis=1, keepdims=True)
    m_prev = m_ref[h]

    log_w = jnp.where(causal, cum_col - cum_row + i_row, NEG_INF)
    log_inter = cum_col + m_prev
    m_row = jnp.maximum(log_inter, jnp.max(log_w, axis=1, keepdims=True))
    w_intra = jnp.exp(log_w - m_row)
    w_inter = jnp.exp(log_inter - m_row)

    v = v_ref[:, vcols].astype(BF16)
    c_state = c_ref[h]
    n_state = n_ref[h]
    attn = w_intra * lax.dot_general(q, k, NT_DIMS, preferred_element_type=F32)
    num = (w_inter * jnp.dot(q, c_state.astype(BF16), preferred_element_type=F32)
           + jnp.dot(attn.astype(BF16), v, preferred_element_type=F32))
    den = (w_inter * jnp.sum(q.astype(F32) * n_state, axis=1, keepdims=True)
           + jnp.sum(attn, axis=1, keepdims=True))
    h_out = num / jnp.maximum(jnp.abs(den), jnp.exp(-m_row))

    lte_col = cum_last - cum_col + i_col
    lte_row = cum_last - cum_row + i_row
    m_new = jnp.maximum(cum_last + m_prev, jnp.max(lte_row, axis=1, keepdims=True))
    w_end = jnp.exp(lte_col - m_new)
    decay = jnp.exp(cum_last + m_prev - m_new)
    kw = k.astype(F32) * w_end
    c_ref[h] = decay * c_state + jnp.dot(kw.T.astype(BF16), v, preferred_element_type=F32)
    n_ref[h] = decay * n_state + jnp.sum(kw, axis=0, keepdims=True)
    m_ref[h] = m_new

    mu = jnp.mean(h_out, axis=1, keepdims=True)
    hc = h_out - mu
    var = jnp.mean(hc * hc, axis=1, keepdims=True)
    h_norm = hc * lax.rsqrt(var + LN_EPS) * ng_ref[:, vcols]
    o_ref[:, vcols] = (_sigmoid(og_ref[:, vcols]) * h_norm).astype(o_ref.dtype)


def mlstm(qkv, conv_w, gates, b_igate, b_fgate, norm_g, *, qk_col0, v_col0, o_col0):
    b, s, _ = qkv.shape
    n_heads = gates.shape[1]
    qk_w = conv_w.shape[1]
    v_w = norm_g.shape[-1]
    dk, dv = qk_w // (2 * n_heads), v_w // n_heads
    chunk = B_CHUNK
    assert qk_col0 % qk_w == 0 and v_col0 % v_w == 0 and o_col0 % v_w == 0 and conv_w.shape[0] <= SUBLANES
    return pl.pallas_call(
        functools.partial(_mlstm_kernel, chunk=chunk, n_heads=n_heads, dk=dk, dv=dv, k_scale=dk ** -0.5),
        grid=(b, s // chunk),
        in_specs=[pl.BlockSpec(memory_space=pltpu.SMEM),
                  pl.BlockSpec(memory_space=pltpu.SMEM),
                  pl.BlockSpec((None, n_heads, 2, chunk), lambda bb, c: (bb, 0, 0, c)),
                  pl.BlockSpec((None, chunk, qk_w), lambda bb, c: (bb, c, qk_col0 // qk_w)),
                  pl.BlockSpec(conv_w.shape, lambda bb, c: (0, 0)),
                  pl.BlockSpec((None, chunk, v_w), lambda bb, c: (bb, c, v_col0 // v_w)),
                  pl.BlockSpec((None, chunk, v_w), lambda bb, c: (bb, c, o_col0 // v_w)),
                  pl.BlockSpec((1, v_w), lambda bb, c: (0, 0))],
        out_specs=pl.BlockSpec((None, chunk, v_w), lambda bb, c: (bb, c, 0)),
        out_shape=jax.ShapeDtypeStruct((b, s, v_w), BF16),
        scratch_shapes=[pltpu.VMEM((SUBLANES + chunk, qk_w), F32), pltpu.VMEM((n_heads, dk, dv), F32),
                        pltpu.VMEM((n_heads, 1, dk), F32), pltpu.VMEM((n_heads, 1, 1), F32)],
        compiler_params=_cparams(("parallel", "arbitrary")),
        name="mlstm",
    )(b_igate, b_fgate, gates, qkv, conv_w, qkv, qkv, norm_g.reshape(1, -1))


MOBA_PENALTY = -3.0e38
MOBA_BIAS_PARTS = 3
MOBA_GROUP = 4
MOBA_KEY_CHUNK = 1024


def _moba_kernel(slopes_ref, q_ref, k_ref, v_ref, o_ref, kaug_ref, vaug_ref, kmean_ref, pen_ref,
                 *, blk, n_blk, top_k, scale):
    h = pl.program_id(1)
    slope = slopes_ref[h]
    dh = q_ref.shape[-1]
    lane = lax.broadcasted_iota(jnp.int32, (blk, LANES), 1)
    row = lax.broadcasted_iota(jnp.int32, (blk, 1), 0)

    kmean_ref[...] = jnp.zeros_like(kmean_ref)
    ones = jnp.ones((blk, dh), BF16)
    for n in range(n_blk):
        rows = slice(n * blk, (n + 1) * blk)
        kb = k_ref[rows, :]
        kmean_ref[n:n + 1, :] = jnp.mean(kb, axis=0, keepdims=True)
        aug = jnp.where(lane == n, 1.0, 0.0)
        rem = slope * (row + n * blk).astype(F32)
        for part in range(MOBA_BIAS_PARTS):
            piece = rem.astype(BF16).astype(F32)
            aug = aug + jnp.where(lane == n_blk + part, piece, 0.0)
            rem = rem - piece
        kaug_ref[rows, 0:dh] = kb.astype(BF16)
        kaug_ref[rows, dh:2 * dh] = aug.astype(BF16)
        vaug_ref[rows, 0:dh] = v_ref[rows, :].astype(BF16)
        vaug_ref[rows, dh:2 * dh] = ones

    ri = lax.broadcasted_iota(jnp.int32, (blk, blk), 0)
    cj = lax.broadcasted_iota(jnp.int32, (blk, blk), 1)

    blk_id = lax.broadcasted_iota(jnp.int32, (n_blk, blk), 0)
    tail_id = lax.broadcasted_iota(jnp.int32, (LANES - n_blk, blk), 0)
    tail = jnp.where(tail_id < MOBA_BIAS_PARTS, 1.0, 0.0)

    def select(qb, carry):
        q_rows = pl.ds(pl.multiple_of(qb * blk, blk), blk)
        gate = lax.dot_general(kmean_ref[0:n_blk, :], q_ref[q_rows, :] * scale, NT_DIMS,
                               precision=lax.Precision.HIGHEST, preferred_element_type=F32)
        gate = jnp.where(blk_id < qb, gate, NEG_INF)
        sel = jnp.zeros(gate.shape, F32)
        for j in range(top_k):
            mx = jnp.max(gate, axis=0, keepdims=True)
            idx = jnp.min(jnp.where(gate == mx, blk_id, n_blk), axis=0, keepdims=True)
            pick = blk_id == idx
            sel = sel + jnp.where(pick, jnp.where(qb > j, 1.0, 0.0), 0.0)
            gate = jnp.where(pick, NEG_INF, gate)
        pen_t = jnp.concatenate([jnp.where(sel > 0.0, 0.0, MOBA_PENALTY), tail], axis=0)
        pen_ref[qb] = pen_t.T.astype(BF16)
        return carry

    lax.fori_loop(0, n_blk, select, 0, unroll=MOBA_GROUP)

    def make_attend(n_keys):
        def attend(qb, carry):
            q_rows = pl.ds(pl.multiple_of(qb * blk, blk), blk)
            q16 = (q_ref[q_rows, :] * scale).astype(BF16)
            pen = pen_ref[qb]
            q_past = jnp.concatenate([q16, pen], axis=1)
            q_own = jnp.concatenate([q16, jnp.where(lane == qb, 0.0, pen.astype(F32)).astype(BF16)], axis=1)
            s_own = lax.dot_general(q_own, kaug_ref[q_rows, :], NT_DIMS, preferred_element_type=F32)
            s_own = jnp.where(ri >= cj, s_own, NEG_INF)
            m = jnp.max(s_own, axis=1, keepdims=True)
            acc = jnp.dot(jnp.exp(s_own - m).astype(BF16), vaug_ref[q_rows, :], preferred_element_type=F32)
            for k0 in range(0, n_keys, MOBA_KEY_CHUNK):
                keys = slice(k0, min(k0 + MOBA_KEY_CHUNK, n_keys))
                s = lax.dot_general(q_past, kaug_ref[keys, :], NT_DIMS, preferred_element_type=F32)
                m_new = jnp.maximum(m, jnp.max(s, axis=1, keepdims=True))
                acc = (jnp.exp(m - m_new) * acc
                       + jnp.dot(jnp.exp(s - m_new).astype(BF16), vaug_ref[keys, :], preferred_element_type=F32))
                m = m_new
            o_ref[q_rows, :] = (acc[:, :dh] / acc[:, dh:]).astype(o_ref.dtype)
            return carry
        return attend

    for first in range(0, n_blk, MOBA_GROUP):
        last = min(first + MOBA_GROUP, n_blk)
        lax.fori_loop(first, last, make_attend(last * blk), 0, unroll=MOBA_GROUP)


def moba_attention(qkv, n_heads, *, q_col, k_col, v_col):
    b, s, _ = qkv.shape
    dh = HEAD_DIM
    blk = MOBA_BLOCK
    n_blk = s // blk
    top_k = min(MOBA_TOPK, n_blk)
    assert n_blk + MOBA_BIAS_PARTS <= LANES and n_blk % SUBLANES == 0 and dh == LANES

    def head(col):
        return pl.BlockSpec((None, s, dh), lambda bb, h: (bb, 0, col + h))

    return pl.pallas_call(
        functools.partial(_moba_kernel, blk=blk, n_blk=n_blk, top_k=top_k, scale=dh ** -0.5),
        grid=(b, n_heads),
        in_specs=[pl.BlockSpec(memory_space=pltpu.SMEM), head(q_col), head(k_col), head(v_col)],
        out_specs=pl.BlockSpec((None, s, dh), lambda bb, h: (bb, 0, h)),
        out_shape=jax.ShapeDtypeStruct((b, s, n_heads * dh), BF16),
        scratch_shapes=[pltpu.VMEM((s, 2 * dh), BF16), pltpu.VMEM((s, 2 * dh), BF16),
                        pltpu.VMEM((LANES, dh), F32), pltpu.VMEM((n_blk, blk, LANES), BF16)],
        compiler_params=_cparams(("parallel", "parallel")),
        name="moba_attention",
    )(_alibi_slopes(n_heads), qkv, qkv, qkv)


CONV_HALO = 32
LN_ROWS = 16


def _conv31_kernel(a_ref, ah_ref, b_ref, bh_ref, w_ref, g_ref, beta_ref, o_ref, glu_ref, conv_ref, shift_ref,
                   *, t_rows, width, channels):
    i = pl.program_id(1)
    halo = ah_ref[...] * _sigmoid(bh_ref[...])
    glu_ref[0:CONV_HALO, :] = jnp.where(i > 0, halo, 0.0)
    glu_ref[CONV_HALO:CONV_HALO + t_rows, :] = a_ref[...] * _sigmoid(b_ref[...])
    off = CONV_HALO - (width - 1)

    strip_rows = CONV_HALO + t_rows - SUBLANES

    def cols(c, carry):
        cs = pl.ds(pl.multiple_of(c * LANES, LANES), LANES)
        for phase in range(1, SUBLANES):
            shift_ref[phase, 0:strip_rows, :] = glu_ref[phase:phase + strip_rows, cs]
        acc = jnp.zeros((t_rows, LANES), F32)
        for tap in range(width):
            phase, base = (off + tap) % SUBLANES, (off + tap) // SUBLANES * SUBLANES
            src = glu_ref[base:base + t_rows, cs] if phase == 0 else shift_ref[phase, base:base + t_rows, :]
            acc = acc + w_ref[tap:tap + 1, cs] * src
        conv_ref[:, cs] = acc
        return carry

    lax.fori_loop(0, channels // LANES, cols, 0)

    def rows(r, carry):
        rs = pl.ds(pl.multiple_of(r * LN_ROWS, LN_ROWS), LN_ROWS)
        y = _ln_rows(conv_ref[rs, :], g_ref[...], beta_ref[...])
        o_ref[rs, :] = (y * _sigmoid(y)).astype(o_ref.dtype)
        return carry

    lax.fori_loop(0, t_rows // LN_ROWS, rows, 0, unroll=2)


def glu_conv_ln_silu(qkv, conv_w, ln_g, ln_b, *, a_col0, b_col0, channels, t_rows):
    b, s, _ = qkv.shape
    width = conv_w.shape[0]
    hb = t_rows // CONV_HALO
    ca, cb = a_col0 // channels, b_col0 // channels

    def cur(col):
        return pl.BlockSpec((None, t_rows, channels), lambda bb, i: (bb, i, col))

    def halo(col):
        return pl.BlockSpec((None, CONV_HALO, channels), lambda bb, i: (bb, jnp.maximum(i * hb - 1, 0), col))

    vec = pl.BlockSpec((1, channels), lambda bb, i: (0, 0))
    return pl.pallas_call(
        functools.partial(_conv31_kernel, t_rows=t_rows, width=width, channels=channels),
        grid=(b, s // t_rows),
        in_specs=[cur(ca), halo(ca), cur(cb), halo(cb),
                  pl.BlockSpec((width, channels), lambda bb, i: (0, 0)), vec, vec],
        out_specs=pl.BlockSpec((None, t_rows, channels), lambda bb, i: (bb, i, 0)),
        out_shape=jax.ShapeDtypeStruct((b, s, channels), BF16),
        scratch_shapes=[pltpu.VMEM((CONV_HALO + t_rows, channels), F32), pltpu.VMEM((t_rows, channels), F32),
                        pltpu.VMEM((SUBLANES, CONV_HALO + t_rows - SUBLANES, LANES), F32)],
        compiler_params=_cparams(("parallel", "parallel")),
        name="glu_conv_ln_silu",
    )(qkv, qkv, qkv, qkv, conv_w, ln_g.reshape(1, -1), ln_b.reshape(1, -1))


DOWN_TILES = dict(tm=2048, tn=1024, tk=1024)
PANEL_TILES = dict(tm=2048, tn=256)
WIDE_PANEL_TILES = dict(tm=2048, tn=512, single_buffer_a=True)
ROW_TILE = 256


def _ffn(u, w_up, w_down):
    hmid = matmul_swiglu(u, w_up, **PANEL_TILES)
    return matmul_f32(hmid, w_down, w_down.shape[1], **DOWN_TILES)


def _mixer_dilated_mlstm(u, bsz, seq, w_in, conv_qk, b_igate, b_fgate, norm_g, w_out):
    d = u.shape[1]
    a_width = d // 2
    a_heads = a_width // HEAD_DIM
    b_width = d - a_width
    qk_width = b_width // 2
    n_main = 3 * a_width + 2 * qk_width + 2 * b_width
    w_in_t = w_in.T
    proj = matmul_rows([u], w_in_t, n_main, w_is_transposed=True, **WIDE_PANEL_TILES).reshape(bsz, seq, n_main)
    gates = matmul_few_cols(u, w_in_t, n_main, 2 * B_HEADS, tm=PANEL_TILES["tm"])
    gates = gates[:, :2 * B_HEADS].reshape(bsz, seq, 2, B_HEADS).transpose(0, 3, 2, 1)

    hb = a_width // HEAD_DIM
    a_out = dilated_attention(proj, a_heads, q_col=0, k_col=hb, v_col=2 * hb)
    qk_col0 = 3 * a_width
    b_out = mlstm(proj, conv_qk, gates, b_igate, b_fgate, norm_g, qk_col0=qk_col0,
                  v_col0=qk_col0 + 2 * qk_width, o_col0=qk_col0 + 2 * qk_width + b_width)
    halves = [t.reshape(bsz * seq, -1) for t in (a_out, b_out)]
    return matmul_rows(halves, w_out, d, **PANEL_TILES)


def _mixer_moba_conv(u, bsz, seq, w_in, conv_dw, conv_ln_g, conv_ln_b, w_out):
    d = u.shape[1]
    c_width = d // 2
    c_heads = c_width // HEAD_DIM
    d_ch = d - c_width
    n_main = 3 * c_width + 2 * d_ch
    proj = matmul_rows([u], w_in, n_main, **WIDE_PANEL_TILES).reshape(bsz, seq, n_main)
    c_out = moba_attention(proj, c_heads, q_col=0, k_col=c_heads, v_col=2 * c_heads)
    d_out = glu_conv_ln_silu(proj, conv_dw, conv_ln_g, conv_ln_b, a_col0=3 * c_width,
                             b_col0=3 * c_width + d_ch, channels=d_ch, t_rows=ROW_TILE)
    halves = [t.reshape(bsz * seq, -1) for t in (c_out, d_out)]
    return matmul_rows(halves, w_out, d, **PANEL_TILES)


def kernel(x, c, l0_w_ada, l0_b_ada, l0_ln_g, l0_ln_b, l0_ffn1_w_up, l0_ffn1_w_down, l0_ffn2_w_up, l0_ffn2_w_down, l0_w_in, l0_conv_qk, l0_b_igate, l0_b_fgate, l0_mlstm_norm_g, l0_w_out, l1_w_ada, l1_b_ada, l1_ln_g, l1_ln_b, l1_ffn1_w_up, l1_ffn1_w_down, l1_ffn2_w_up, l1_ffn2_w_down, l1_w_in, l1_conv_dw, l1_conv_ln_g, l1_conv_ln_b, l1_w_out):
    bsz, seq, d = x.shape
    depth = 2
    alpha = (2 * depth) ** 0.25
    layers = (
        (l0_w_ada, l0_b_ada, l0_ln_g, l0_ln_b, l0_ffn1_w_up, l0_ffn1_w_down, l0_ffn2_w_up, l0_ffn2_w_down),
        (l1_w_ada, l1_b_ada, l1_ln_g, l1_ln_b, l1_ffn1_w_up, l1_ffn1_w_down, l1_ffn2_w_up, l1_ffn2_w_down),
    )
    mixers = (
        lambda u: _mixer_dilated_mlstm(u, bsz, seq, l0_w_in, l0_conv_qk, l0_b_igate, l0_b_fgate,
                                       l0_mlstm_norm_g, l0_w_out),
        lambda u: _mixer_moba_conv(u, bsz, seq, l1_w_in, l1_conv_dw, l1_conv_ln_g, l1_conv_ln_b, l1_w_out),
    )
    c_pad = jnp.pad(c, ((0, SUBLANES - bsz), (0, 0)))
    mods = [ada_modulation(c_pad, p[0], p[1], tn=512)[:bsz].reshape(bsz, 3 * N_SUBLAYERS, 1, d) for p in layers]

    x2 = x.reshape(bsz * seq, d)
    u = modulate(x2, mods[0], 0, 1, rows_per_batch=seq, tm=ROW_TILE)
    for layer in range(depth):
        _, _, ln_g, ln_b, f1_up, f1_down, f2_up, f2_down = layers[layer]
        branches = (
            (0.5, lambda v: _ffn(v, f1_up, f1_down)),
            (1.0, mixers[layer]),
            (0.5, lambda v: _ffn(v, f2_up, f2_down)),
        )
        for j, (res_w, fn) in enumerate(branches):
            y = fn(u)
            if j + 1 < N_SUBLAYERS:
                nxt, nj = mods[layer], (3 * (j + 1), 3 * (j + 1) + 1)
            elif layer + 1 < depth:
                nxt, nj = mods[layer + 1], (0, 1)
            else:
                nxt, nj = None, None
            x2, u = ln_residual(x2, y, mods[layer], 3 * j + 2, ln_g, ln_b, j, res_w, alpha, nxt, nj,
                                rows_per_batch=seq, tm=ROW_TILE)
    return x2.reshape(bsz, seq, d)
```
